```python
import jax, jax.numpy as jnp
from jax import lax
import numpy as np

D_MODEL = 1024
BATCH = 8
SEQ = 4096
DEPTH = 4

CHUNK = 64
D_MIX = D_MODEL
GM_WIDTH = D_MIX // 2
GM_HEADS = 8
GM_HEAD_DIM = GM_WIDTH // GM_HEADS
GM_BLOCK = 128
SB_WIDTH = D_MIX - GM_WIDTH
SB_HEADS = 8
SB_HEAD_DIM = SB_WIDTH // SB_HEADS
SB_BLOCK = 128
IN_COLS = 2 * GM_WIDTH + 3 * SB_WIDTH
PEER_HEADS = 8
PEER_QUERY_DIM = 256
PEER_HALF = PEER_QUERY_DIM // 2
N_KEYS = 128
N_EXPERTS = N_KEYS * N_KEYS
PEER_TOPK = 16
PEER_BLOCK = 128
ADA_SCALE = 0.5
EPS = 1e-6

kernel_name = "hybrid_gmlp_stickbreaking_peer_adaln"


def rmsnorm(x, g):
    xf = x.astype(jnp.float32)
    y = xf * lax.rsqrt(jnp.mean(xf * xf, axis=-1, keepdims=True) + EPS)
    return (y * g.astype(jnp.float32)).astype(x.dtype)


def spatial_gating(u, v, ws, bs, v_gain):
    B, S, _ = v.shape
    pos_chunk = jnp.arange(GM_BLOCK) // CHUNK
    mask = pos_chunk[None, :] <= pos_chunk[:, None]
    w = jnp.where(mask[None], ws, jnp.zeros_like(ws))
    vh = v.reshape(B, S // GM_BLOCK, GM_BLOCK, GM_HEADS, GM_HEAD_DIM)
    vh = rmsnorm(vh, v_gain.reshape(GM_HEADS, GM_HEAD_DIM))
    z = jnp.einsum('hts,bnshc->bnthc', w, vh) + bs.T[None, None, :, :, None]
    return u * z.reshape(B, S, GM_WIDTH)


def stick_breaking(q, k, v):
    B, S, H, dh = q.shape
    scale = 1.0 / np.sqrt(dh)
    outs = []
    for i in range(S // SB_BLOCK):
        t0, t1 = i * SB_BLOCK, (i + 1) * SB_BLOCK
        qb, kb, vb = q[:, t0:t1], k[:, :t1], v[:, :t1]
        z = jnp.einsum('bthd,bshd->bhts', qb, kb).astype(jnp.float32) * scale
        t_idx = t0 + jnp.arange(SB_BLOCK)[:, None]
        s_idx = jnp.arange(t1)[None, :]
        causal = s_idx < t_idx
        log_beta = jax.nn.log_sigmoid(z)
        log_keep = jnp.where(causal, jax.nn.log_sigmoid(-z), 0.0)
        later = lax.cumsum(log_keep, axis=3, reverse=True) - log_keep
        a = jnp.where(causal, jnp.exp(log_beta + later), 0.0)
        outs.append(jnp.einsum('bhts,bshd->bthd', a.astype(vb.dtype), vb))
    return jnp.concatenate(outs, axis=1)


def mixer(h, w_in, gm_ws, gm_bs, gm_vnorm, out_norm_a, out_norm_b, w_out):
    B, S, _ = h.shape
    proj = h @ w_in
    a_u, a_v, sq, sk, sv = jnp.split(
        proj, [GM_WIDTH, 2 * GM_WIDTH, 2 * GM_WIDTH + SB_WIDTH, 2 * GM_WIDTH + 2 * SB_WIDTH], axis=-1)
    y_a = spatial_gating(jax.nn.gelu(a_u), jax.nn.gelu(a_v), gm_ws, gm_bs, gm_vnorm)
    hs = (B, S, SB_HEADS, SB_HEAD_DIM)
    y_b = stick_breaking(sq.reshape(hs), sk.reshape(hs), sv.reshape(hs)).reshape(B, S, SB_WIDTH)
    y = jnp.concatenate([rmsnorm(y_a, out_norm_a), rmsnorm(y_b, out_norm_b)], axis=-1)
    return y @ w_out


def peer_ffn(h, wq, k1, k2, u_tab, v_tab):
    B, S, D = h.shape
    hb = h.reshape((B * S) // PEER_BLOCK, PEER_BLOCK, D)

    def block(xb):
        q = (xb @ wq).reshape(PEER_BLOCK, PEER_HEADS, 2, PEER_HALF)
        s1 = jnp.einsum('phd,kd->phk', q[:, :, 0], k1).astype(jnp.float32)
        s2 = jnp.einsum('phd,kd->phk', q[:, :, 1], k2).astype(jnp.float32)
        v1, i1 = lax.top_k(s1, PEER_TOPK)
        v2, i2 = lax.top_k(s2, PEER_TOPK)
        cand = (v1[..., :, None] + v2[..., None, :]).reshape(PEER_BLOCK, PEER_HEADS, PEER_TOPK * PEER_TOPK)
        cand_id = (i1[..., :, None] * N_KEYS + i2[..., None, :]).reshape(PEER_BLOCK, PEER_HEADS, PEER_TOPK * PEER_TOPK)
        top_s, top_pos = lax.top_k(cand, PEER_TOPK)
        eid = jnp.take_along_axis(cand_id, top_pos, axis=-1)
        g = jax.nn.softmax(top_s, axis=-1)
        ue = u_tab[eid]
        ve = v_tab[eid]
        act = jax.nn.gelu(jnp.einsum('pd,phkd->phk', xb, ue))
        return jnp.einsum('phk,phkd->pd', g.astype(xb.dtype) * act, ve)

    return lax.map(block, hb).reshape(B, S, D)


def setup_inputs(seed: int = 0) -> dict:
    key = jax.random.key(seed)
    ks = jax.random.split(key, 20)
    f32 = jnp.float32
    nrm = lambda k, shape, s: jax.random.normal(k, shape, f32) * s
    L, D = DEPTH, D_MODEL
    return {
        "x": nrm(ks[0], (BATCH, SEQ, D), 1.0),
        "c": nrm(ks[1], (BATCH, D), 1.0),
        "ada_w": nrm(ks[2], (L, D, 6 * D), ADA_SCALE * D ** -0.5),
        "ada_b": nrm(ks[3], (L, 6 * D), 0.02),
        "norm_mix": 1.0 + nrm(ks[4], (L, D), 0.05),
        "norm_ffn": 1.0 + nrm(ks[5], (L, D), 0.05),
        "w_in": nrm(ks[6], (L, D, IN_COLS), D ** -0.5),
        "gm_ws": nrm(ks[7], (L, GM_HEADS, GM_BLOCK, GM_BLOCK), GM_BLOCK ** -0.5),
        "gm_bs": 1.0 + nrm(ks[8], (L, GM_HEADS, GM_BLOCK), 0.1),
        "gm_vnorm": 1.0 + nrm(ks[9], (L, GM_WIDTH), 0.05),
        "out_norm_a": 1.0 + nrm(ks[10], (L, GM_WIDTH), 0.05),
        "out_norm_b": 1.0 + nrm(ks[11], (L, SB_WIDTH), 0.05),
        "w_out": nrm(ks[12], (L, D_MIX, D), D_MIX ** -0.5),
        "peer_wq": nrm(ks[13], (L, D, PEER_HEADS * PEER_QUERY_DIM), D ** -0.5),
        "peer_k1": nrm(ks[14], (L, N_KEYS, PEER_HALF), PEER_HALF ** -0.5),
        "peer_k2": nrm(ks[15], (L, N_KEYS, PEER_HALF), PEER_HALF ** -0.5),
        "peer_u": nrm(ks[16], (L, N_EXPERTS, D), D ** -0.5),
        "peer_v": nrm(ks[17], (L, N_EXPERTS, D), PEER_HEADS ** -0.5),
        "final_norm": 1.0 + nrm(ks[18], (D,), 0.05),
    }


def reference(x, c, ada_w, ada_b, norm_mix, norm_ffn, w_in, gm_ws, gm_bs, gm_vnorm,
              out_norm_a, out_norm_b, w_out, peer_wq, peer_k1, peer_k2, peer_u, peer_v,
              final_norm):
    c_act = jax.nn.silu(c)
    for l in range(DEPTH):
        mod = c_act @ ada_w[l] + ada_b[l]
        sh1, sc1, g1, sh2, sc2, g2 = jnp.split(mod[:, None, :], 6, axis=-1)
        h = rmsnorm(x, norm_mix[l]) * (1.0 + sc1) + sh1
        x = x + g1 * mixer(h, w_in[l], gm_ws[l], gm_bs[l], gm_vnorm[l],
                           out_norm_a[l], out_norm_b[l], w_out[l])
        h = rmsnorm(x, norm_ffn[l]) * (1.0 + sc2) + sh2
        x = x + g2 * peer_ffn(h, peer_wq[l], peer_k1[l], peer_k2[l], peer_u[l], peer_v[l])
    return rmsnorm(x, final_norm)
```

```python
import functools

import jax
import jax.numpy as jnp
import numpy as np
from jax import lax
from jax.experimental import pallas as pl
from jax.experimental.pallas import tpu as pltpu
from jax.experimental.pallas import tpu_sc as plsc

EPS = 1e-6
CHUNK = 64
GM_HEADS = 8
GM_BLOCK = 128
SB_HEADS = 8
HEAD_DIM = 64
PEER_HEADS = 8
PEER_HALF = 128
N_KEYS = 128
TOPK = 16
PICKS = PEER_HEADS * TOPK

LANES = 128
VMEM_LIMIT = 56 * 1024 * 1024

MIX_TOKENS = 512
SB_Q = 256
SB_K = 256
OUT_TOKENS = 256
PEER_TOKENS = 16
PEER_CHUNK = 4096
SC_ROWS = 64

NEG = -1e30


def _gelu(x):
    return 0.5 * x * (1.0 + jnp.tanh(np.sqrt(2.0 / np.pi) * (x + 0.044715 * (x * x * x))))


def _rms(x, gain):
    return x * lax.rsqrt(jnp.mean(x * x, axis=-1, keepdims=True) + EPS) * gain


def _mod_kernel(c_ref, w_ref, b_ref, o_ref):
    c = c_ref[...]
    c_act = c * jax.nn.sigmoid(c)
    o_ref[...] = jnp.dot(c_act, w_ref[...], preferred_element_type=jnp.float32) + b_ref[...]


def _ada_mod(c, ada_w, ada_b):
    depth, d, cols = ada_w.shape
    b = c.shape[0]
    tn = 1536
    return pl.pallas_call(
        _mod_kernel,
        grid=(depth, cols // tn),
        in_specs=[
            pl.BlockSpec((b, d), lambda l, j: (0, 0)),
            pl.BlockSpec((None, d, tn), lambda l, j: (l, 0, j)),
            pl.BlockSpec((None, 1, tn), lambda l, j: (l, 0, j)),
        ],
        out_specs=pl.BlockSpec((None, b, tn), lambda l, j: (l, 0, j)),
        out_shape=jax.ShapeDtypeStruct((depth, b, cols), jnp.float32),
        compiler_params=pltpu.CompilerParams(vmem_limit_bytes=VMEM_LIMIT),
        name="ada_mod",
    )(c, ada_w, ada_b.reshape(depth, 1, cols))


def _mix_in_kernel(x_ref, mod_ref, nm_ref, win_ref, ws_ref, bst_ref, vn_ref, na_ref,
                   ya_ref, q_ref, k_ref, v_ref):
    gw = GM_HEADS * HEAD_DIM
    x = x_ref[...]
    mod = mod_ref[...]
    h = _rms(x, nm_ref[...]) * (1.0 + mod[1:2, :]) + mod[0:1, :]
    proj = jnp.dot(h.astype(jnp.bfloat16), win_ref[...], preferred_element_type=jnp.float32)
    u = _gelu(proj[:, 0:gw])
    v = _gelu(proj[:, gw:2 * gw])
    q_ref[...] = proj[:, 2 * gw:3 * gw].astype(jnp.bfloat16)
    k_ref[...] = proj[:, 3 * gw:4 * gw].astype(jnp.bfloat16)
    v_ref[...] = proj[:, 4 * gw:5 * gw].astype(jnp.bfloat16)

    r = lax.broadcasted_iota(jnp.int32, (gw, gw), 0) // HEAD_DIM
    cc = lax.broadcasted_iota(jnp.int32, (gw, gw), 1) // HEAD_DIM
    avg = jnp.where(r == cc, 1.0 / HEAD_DIM, 0.0).astype(jnp.bfloat16)
    ms = jnp.dot((v * v).astype(jnp.bfloat16), avg, preferred_element_type=jnp.float32)
    vh = (v * lax.rsqrt(ms + EPS) * vn_ref[...]).astype(jnp.bfloat16)

    t_chunk = lax.broadcasted_iota(jnp.int32, (GM_BLOCK, GM_BLOCK), 0) // CHUNK
    s_chunk = lax.broadcasted_iota(jnp.int32, (GM_BLOCK, GM_BLOCK), 1) // CHUNK
    keep = s_chunk <= t_chunk
    lane = lax.broadcasted_iota(jnp.int32, (GM_BLOCK, LANES), 1)
    first = lane < HEAD_DIM
    bst = bst_ref[...]
    n_blocks = x.shape[0] // GM_BLOCK
    for p in range(GM_HEADS // 2):
        w0 = jnp.where(keep, ws_ref[2 * p], 0.0).astype(jnp.bfloat16)
        w1 = jnp.where(keep, ws_ref[2 * p + 1], 0.0).astype(jnp.bfloat16)
        bias = jnp.where(first, bst[:, 2 * p:2 * p + 1], bst[:, 2 * p + 1:2 * p + 2])
        for nb in range(n_blocks):
            rows = slice(nb * GM_BLOCK, (nb + 1) * GM_BLOCK)
            cols = slice(p * LANES, (p + 1) * LANES)
            vp = vh[rows, cols]
            z0 = jnp.dot(w0, vp, preferred_element_type=jnp.float32)
            z1 = jnp.dot(w1, vp, preferred_element_type=jnp.float32)
            z = jnp.where(first, z0, z1) + bias
            ya_ref[rows, cols] = u[rows, cols] * z
    ya = ya_ref[...]
    ya_ref[...] = _rms(ya, na_ref[...])


def _mix_in(x2, mod_l, norm_mix, w_in_bf, gm_ws, gm_bst, gm_vnorm, out_norm_a, seq):
    ntok, d = x2.shape
    gw = GM_HEADS * HEAD_DIM
    ts = min(MIX_TOKENS, seq)
    per_b = seq // ts
    tok_spec = lambda w: pl.BlockSpec((ts, w), lambda i: (i, 0))
    full = lambda a: pl.BlockSpec(a.shape, lambda i: (0,) * a.ndim)
    outs = pl.pallas_call(
        _mix_in_kernel,
        grid=(ntok // ts,),
        in_specs=[
            tok_spec(d),
            pl.BlockSpec((None, 6, d), lambda i: (i // per_b, 0, 0)),
            full(norm_mix), full(w_in_bf), full(gm_ws), full(gm_bst), full(gm_vnorm), full(out_norm_a),
        ],
        out_specs=[tok_spec(gw), tok_spec(gw), tok_spec(gw), tok_spec(gw)],
        out_shape=[
            jax.ShapeDtypeStruct((ntok, gw), jnp.float32),
            jax.ShapeDtypeStruct((ntok, gw), jnp.bfloat16),
            jax.ShapeDtypeStruct((ntok, gw), jnp.bfloat16),
            jax.ShapeDtypeStruct((ntok, gw), jnp.bfloat16),
        ],
        compiler_params=pltpu.CompilerParams(vmem_limit_bytes=VMEM_LIMIT),
        name="mix_in",
    )(x2, mod_l, norm_mix, w_in_bf, gm_ws, gm_bst, gm_vnorm, out_norm_a)
    return outs


def _sb_block(qh, k, v, tri, carry, mask):
    z = lax.dot_general(qh, k, (((1,), (1,)), ((), ())), preferred_element_type=jnp.float32)
    soft = jnp.log(1.0 + jnp.exp(-jnp.abs(z)))
    log_beta = jnp.minimum(z, 0.0) - soft
    log_keep = log_beta - z
    if mask is not None:
        log_keep = jnp.where(mask, log_keep, 0.0)
    hi = log_keep.astype(jnp.bfloat16)
    lo = (log_keep - hi.astype(jnp.float32)).astype(jnp.bfloat16)
    later = (jnp.dot(hi, tri, preferred_element_type=jnp.float32)
             + jnp.dot(lo, tri, preferred_element_type=jnp.float32))
    a = jnp.exp(log_beta + later + carry)
    if mask is not None:
        a = jnp.where(mask, a, 0.0)
    out = jnp.dot(a.astype(jnp.bfloat16), v, preferred_element_type=jnp.float32)
    return out, carry + later[:, 0:1] + log_keep[:, 0:1]


def _sb_kernel(q_ref, k_ref, v_ref, o_ref, acc_ref, car_ref):
    tq, tk = q_ref.shape[0], SB_K
    qi = pl.program_id(2)
    lane = lax.broadcasted_iota(jnp.int32, (tq, LANES), 1)
    first = lane < HEAD_DIM
    q = q_ref[...] * (1.0 / np.sqrt(HEAD_DIM))
    zero = jnp.zeros_like(q)
    qs = (jnp.where(first, q, zero), jnp.where(first, zero, q))
    tri = (lax.broadcasted_iota(jnp.int32, (tk, tk), 0)
           > lax.broadcasted_iota(jnp.int32, (tk, tk), 1)).astype(jnp.bfloat16)
    causal = (lax.broadcasted_iota(jnp.int32, (tq, tk), 1)
              < lax.broadcasted_iota(jnp.int32, (tq, tk), 0))

    start = pl.multiple_of(qi * tk, tk)
    kd = k_ref[pl.ds(start, tk), :]
    vd = v_ref[pl.ds(start, tk), :]
    for hh in range(2):
        out, car = _sb_block(qs[hh], kd, vd, tri, jnp.zeros((tq, 1), jnp.float32), causal)
        acc_ref[hh] = out
        car_ref[hh] = jnp.broadcast_to(car, (tq, LANES))

    def body(i, _):
        off = pl.multiple_of((qi - 1 - i) * tk, tk)
        kb = k_ref[pl.ds(off, tk), :]
        vb = v_ref[pl.ds(off, tk), :]
        for hh in range(2):
            out, car = _sb_block(qs[hh], kb, vb, tri, car_ref[hh][:, 0:1], None)
            acc_ref[hh] += out
            car_ref[hh] = jnp.broadcast_to(car, (tq, LANES))
        return 0

    lax.fori_loop(0, qi, body, 0)
    o_ref[...] = jnp.where(first, acc_ref[0], acc_ref[1])


def _stick_break(q, k, v, batch, seq):
    ntok, w = q.shape
    pairs = w // LANES
    tq = min(SB_Q, seq)
    nq = seq // tq
    q_spec = pl.BlockSpec((tq, LANES), lambda b, p, i: (b * nq + i, p))
    kv_spec = pl.BlockSpec((seq, LANES), lambda b, p, i: (b, p))
    return pl.pallas_call(
        _sb_kernel,
        grid=(batch, pairs, nq),
        in_specs=[q_spec, kv_spec, kv_spec],
        out_specs=pl.BlockSpec((tq, LANES), lambda b, p, i: (b * nq + i, p)),
        out_shape=jax.ShapeDtypeStruct((ntok, w), jnp.float32),
        scratch_shapes=[pltpu.VMEM((2, tq, LANES), jnp.float32),
                        pltpu.VMEM((2, tq, LANES), jnp.float32)],
        compiler_params=pltpu.CompilerParams(vmem_limit_bytes=VMEM_LIMIT),
        name="stick_break",
    )(q, k, v)


def _top_rows(s, k):
    c, t = s.shape
    row = lax.broadcasted_iota(jnp.int32, (c, t), 0).astype(jnp.float32)
    slot = lax.broadcasted_iota(jnp.int32, (k, t), 0)
    vals = jnp.zeros((k, t), jnp.float32)
    idxs = jnp.zeros((k, t), jnp.float32)
    for r in range(k):
        m = jnp.max(s, axis=0, keepdims=True)
        idx = jnp.min(jnp.where(s == m, row, float(c)), axis=0, keepdims=True)
        vals = jnp.where(slot == r, m, vals)
        idxs = jnp.where(slot == r, idx, idxs)
        s = jnp.where(row == idx, NEG, s)
    return vals, idxs


def _mix_out_kernel(x_ref, ya_ref, yb_ref, mod_ref, nb_ref, wout_ref, nf_ref, wq_ref, k1_ref, k2_ref,
                    x1_ref, h2_ref, eid_ref, gate_ref, q_scr):
    gw = ya_ref.shape[1]
    mod = mod_ref[...]
    ya = ya_ref[...].astype(jnp.bfloat16)
    yb = _rms(yb_ref[...], nb_ref[...]).astype(jnp.bfloat16)
    y = (jnp.dot(ya, wout_ref[0:gw, :], preferred_element_type=jnp.float32)
         + jnp.dot(yb, wout_ref[gw:2 * gw, :], preferred_element_type=jnp.float32))
    x1 = x_ref[...] + mod[2:3, :] * y
    x1_ref[...] = x1
    h2 = _rms(x1, nf_ref[...]) * (1.0 + mod[4:5, :]) + mod[3:4, :]
    h2_ref[...] = h2
    q_scr[...] = jnp.dot(h2.astype(jnp.bfloat16), wq_ref[...], preferred_element_type=jnp.float32)

    k1 = k1_ref[...].astype(jnp.bfloat16)
    k2 = k2_ref[...].astype(jnp.bfloat16)
    t = x1.shape[0]
    slot = lax.broadcasted_iota(jnp.int32, (TOPK, t), 0)
    slot_f = slot.astype(jnp.float32)
    nt = (((1,), (1,)), ((), ()))

    def head(h, _):
        c0 = pl.multiple_of(h * 2 * PEER_HALF, 2 * PEER_HALF)
        qa = q_scr[:, pl.ds(c0, PEER_HALF)].astype(jnp.bfloat16)
        qb = q_scr[:, pl.ds(c0 + PEER_HALF, PEER_HALF)].astype(jnp.bfloat16)
        s1 = lax.dot_general(k1, qa, nt, preferred_element_type=jnp.float32)
        s2 = lax.dot_general(k2, qb, nt, preferred_element_type=jnp.float32)
        v1, i1 = _top_rows(s1, TOPK)
        v2, i2 = _top_rows(s2, TOPK)
        cand = jnp.concatenate([v1[i:i + 1, :] + v2 for i in range(TOPK)], axis=0)
        top_s, pos = _top_rows(cand, TOPK)
        eid = jnp.zeros((TOPK, t), jnp.float32)
        for r in range(TOPK):
            pr = pos[r:r + 1, :]
            pi = jnp.floor(pr * (1.0 / TOPK))
            pj = pr - pi * TOPK
            e1 = jnp.sum(jnp.where(slot_f == pi, i1, 0.0), axis=0, keepdims=True)
            e2 = jnp.sum(jnp.where(slot_f == pj, i2, 0.0), axis=0, keepdims=True)
            eid = jnp.where(slot == r, e1 * N_KEYS + e2, eid)
        ex = jnp.exp(top_s - top_s[0:1, :])
        gate = ex / jnp.sum(ex, axis=0, keepdims=True)
        r0 = pl.multiple_of(h * TOPK, TOPK)
        eid_ref[pl.ds(r0, TOPK), :] = eid.astype(jnp.int32)
        gate_ref[pl.ds(r0, TOPK), :] = gate
        return 0

    lax.fori_loop(0, PEER_HEADS, head, 0)


def _mix_out(x2, ya, yb, mod_l, out_norm_b, w_out_bf, norm_ffn, wq_bf, k1, k2, seq):
    ntok, d = x2.shape
    gw = ya.shape[1]
    tt = min(OUT_TOKENS, seq)
    per_b = seq // tt
    qcols = wq_bf.shape[1]
    tok_spec = lambda w: pl.BlockSpec((tt, w), lambda i: (i, 0))
    full = lambda a: pl.BlockSpec(a.shape, lambda i: (0,) * a.ndim)
    pick_spec = pl.BlockSpec((PICKS, tt), lambda i: (0, i))
    return pl.pallas_call(
        _mix_out_kernel,
        grid=(ntok // tt,),
        in_specs=[
            tok_spec(d), tok_spec(gw), tok_spec(gw),
            pl.BlockSpec((None, 6, d), lambda i: (i // per_b, 0, 0)),
            full(out_norm_b), full(w_out_bf), full(norm_ffn), full(wq_bf), full(k1), full(k2),
        ],
        out_specs=[tok_spec(d), tok_spec(d), pick_spec, pick_spec],
        out_shape=[
            jax.ShapeDtypeStruct((ntok, d), jnp.float32),
            jax.ShapeDtypeStruct((ntok, d), jnp.float32),
            jax.ShapeDtypeStruct((PICKS, ntok), jnp.int32),
            jax.ShapeDtypeStruct((PICKS, ntok), jnp.float32),
        ],
        scratch_shapes=[pltpu.VMEM((tt, qcols), jnp.float32)],
        compiler_params=pltpu.CompilerParams(vmem_limit_bytes=VMEM_LIMIT),
        name="mix_out",
    )(x2, ya, yb, mod_l, out_norm_b, w_out_bf, norm_ffn, wq_bf, k1, k2)


def _gather_rows(table, idx):
    n = idx.shape[0]
    width = table.shape[1]
    info = plsc.get_sparse_core_info()
    workers = info.num_cores * info.num_subcores
    per_worker = n // workers
    steps = per_worker // SC_ROWS
    assert per_worker * workers == n and steps * SC_ROWS == per_worker
    mesh = plsc.VectorSubcoreMesh(core_axis_name="c", subcore_axis_name="s")

    @functools.partial(
        pl.kernel, mesh=mesh,
        out_type=jax.ShapeDtypeStruct((n, width), table.dtype),
        scratch_types=[
            pltpu.VMEM((SC_ROWS,), jnp.int32),
            pltpu.VMEM((SC_ROWS, width), table.dtype),
            pltpu.SemaphoreType.DMA,
        ],
    )
    def gather(table_hbm, idx_hbm, out_hbm, idx_v, rows_v, sem):
        wid = lax.axis_index("s") * info.num_cores + lax.axis_index("c")
        base = wid * per_worker

        @pl.loop(0, steps)
        def _(i):
            off = pl.multiple_of(base + i * SC_ROWS, SC_ROWS)
            pltpu.sync_copy(idx_hbm.at[pl.ds(off, SC_ROWS)], idx_v)
            pltpu.async_copy(table_hbm.at[idx_v], rows_v, sem).wait()
            pltpu.sync_copy(rows_v, out_hbm.at[pl.ds(off, SC_ROWS)])

    return gather(table, idx)


def _unpack(words):
    lo = lax.bitcast_convert_type(lax.shift_left(words, 16), jnp.float32)
    hi = lax.bitcast_convert_type(jnp.bitwise_and(words, jnp.int32(-65536)), jnp.float32)
    return lo, hi


def _peer_kernel(x1_ref, h2_ref, gate_ref, u_ref, v_ref, mod_ref, o_ref):
    half = u_ref.shape[1]
    tokens = x1_ref.shape[0]
    gates = gate_ref[...]
    for t in range(tokens):
        rows = slice(t * PICKS, (t + 1) * PICKS)
        h = h2_ref[t:t + 1, :]
        u_lo, u_hi = _unpack(u_ref[rows, :])
        prod = u_lo * h[:, 0:half] + u_hi * h[:, half:2 * half]
        act = _gelu(jnp.sum(prod, axis=1, keepdims=True))
        wgt = gates[:, t:t + 1] * act
        v_lo, v_hi = _unpack(v_ref[rows, :])
        o_ref[t:t + 1, 0:half] = jnp.sum(wgt * v_lo, axis=0, keepdims=True)
        o_ref[t:t + 1, half:2 * half] = jnp.sum(wgt * v_hi, axis=0, keepdims=True)
    o_ref[...] = x1_ref[...] + mod_ref[5:6, :] * o_ref[...]


def _peer_apply(x1c, h2c, gates_c, u_rows, v_rows, mod_l, seq, tok0):
    ntok, d = x1c.shape
    half = d // 2
    tp = PEER_TOKENS
    tok_spec = pl.BlockSpec((tp, d), lambda i: (i, 0))
    row_spec = pl.BlockSpec((tp * PICKS, half), lambda i: (i, 0))
    return pl.pallas_call(
        _peer_kernel,
        grid=(ntok // tp,),
        in_specs=[
            tok_spec, tok_spec,
            pl.BlockSpec((None, PICKS, tp), lambda i: (i, 0, 0)),
            row_spec, row_spec,
            pl.BlockSpec((None, 6, d), lambda i: ((tok0 + i * tp) // seq, 0, 0)),
        ],
        out_specs=tok_spec,
        out_shape=jax.ShapeDtypeStruct((ntok, d), jnp.float32),
        compiler_params=pltpu.CompilerParams(vmem_limit_bytes=VMEM_LIMIT),
        name="peer_apply",
    )(x1c, h2c, gates_c, u_rows, v_rows, mod_l)


def _final_kernel(x_ref, g_ref, o_ref):
    o_ref[...] = _rms(x_ref[...], g_ref[...])


def _final_norm(x2, gain):
    ntok, d = x2.shape
    tt = 1024 if ntok % 1024 == 0 else ntok
    return pl.pallas_call(
        _final_kernel,
        grid=(ntok // tt,),
        in_specs=[pl.BlockSpec((tt, d), lambda i: (i, 0)), pl.BlockSpec((1, d), lambda i: (0, 0))],
        out_specs=pl.BlockSpec((tt, d), lambda i: (i, 0)),
        out_shape=jax.ShapeDtypeStruct((ntok, d), jnp.float32),
        name="final_norm",
    )(x2, gain)


def _pack_table(tab):
    half = tab.shape[1] // 2
    bits = lax.bitcast_convert_type(tab.astype(jnp.bfloat16), jnp.uint16).astype(jnp.uint32)
    words = bits[:, :half] | (bits[:, half:] << 16)
    return lax.bitcast_convert_type(words, jnp.int32)


def kernel(x, c, ada_w, ada_b, norm_mix, norm_ffn, w_in, gm_ws, gm_bs, gm_vnorm, out_norm_a, out_norm_b,
           w_out, peer_wq, peer_k1, peer_k2, peer_u, peer_v, final_norm):
    batch, seq, d = x.shape
    depth = ada_w.shape[0]
    ntok = batch * seq
    bf = jnp.bfloat16
    mod = _ada_mod(c, ada_w, ada_b).reshape(depth, batch, 6, d)
    x2 = x.reshape(ntok, d)
    chunk = min(PEER_CHUNK, ntok)
    for l in range(depth):
        row = lambda a: a[l].reshape(1, -1)
        ya, q, k, v = _mix_in(x2, mod[l], row(norm_mix), w_in[l].astype(bf), gm_ws[l], gm_bs[l].T,
                              row(gm_vnorm), row(out_norm_a), seq)
        yb = _stick_break(q, k, v, batch, seq)
        x1, h2, eid, gate = _mix_out(x2, ya, yb, mod[l], row(out_norm_b), w_out[l].astype(bf), row(norm_ffn),
                                     peer_wq[l].astype(bf), peer_k1[l], peer_k2[l], seq)
        u_pack = _pack_table(peer_u[l])
        v_pack = _pack_table(peer_v[l])
        eid_t = eid.T.reshape(ntok * PICKS)
        gate_t = gate.reshape(PICKS, ntok // PEER_TOKENS, PEER_TOKENS).transpose(1, 0, 2)
        outs = []
        for t0 in range(0, ntok, chunk):
            idx = lax.slice(eid_t, (t0 * PICKS,), ((t0 + chunk) * PICKS,))
            u_rows = _gather_rows(u_pack, idx)
            v_rows = _gather_rows(v_pack, idx)
            g0 = t0 // PEER_TOKENS
            outs.append(_peer_apply(x1[t0:t0 + chunk], h2[t0:t0 + chunk],
                                    gate_t[g0:g0 + chunk // PEER_TOKENS], u_rows, v_rows, mod[l], seq, t0))
        x2 = jnp.concatenate(outs, axis=0) if len(outs) > 1 else outs[0]
    return _final_norm(x2, final_norm.reshape(1, d)).reshape(batch, seq, d)
```

```python
import functools

import jax
import jax.numpy as jnp
import numpy as np
from jax import lax
from jax.experimental import pallas as pl
from jax.experimental.pallas import tpu as pltpu
from jax.experimental.pallas import tpu_sc as plsc

EPS = 1e-6
CHUNK = 64
GM_HEADS = 8
GM_BLOCK = 128
HEAD_DIM = 64
PEER_HEADS = 8
PEER_HALF = 128
N_KEYS = 128
TOPK = 16
PICKS = PEER_HEADS * TOPK

LANES = 128
SUBLANES = 8
VMEM_LIMIT = 56 * 1024 * 1024

MIX_TOKENS = 512
SB_Q = 256
SB_K = 256
OUT_TOKENS = 256
PEER_TOKENS = 16
SC_ROWS = 64


def _gelu(x):
    return 0.5 * x * (1.0 + jnp.tanh(np.sqrt(2.0 / np.pi) * (x + 0.044715 * (x * x * x))))


def _rms(x, gain):
    return x * lax.rsqrt(jnp.mean(x * x, axis=-1, keepdims=True) + EPS) * gain


def _full(a):
    return pl.BlockSpec(a.shape, lambda *_: (0,) * a.ndim)


def _mod_kernel(c_ref, w_ref, b_ref, o_ref):
    c = c_ref[...]
    c_act = c * jax.nn.sigmoid(c)
    o_ref[...] = jnp.dot(c_act, w_ref[...], preferred_element_type=jnp.float32) + b_ref[...]


def _ada_mod(c, ada_w, ada_b):
    depth, d, cols = ada_w.shape
    b = c.shape[0]
    tn = cols // 4
    return pl.pallas_call(
        _mod_kernel,
        grid=(depth, cols // tn),
        in_specs=[
            pl.BlockSpec((b, d), lambda l, j: (0, 0)),
            pl.BlockSpec((None, d, tn), lambda l, j: (l, 0, j)),
            pl.BlockSpec((None, 1, tn), lambda l, j: (l, 0, j)),
        ],
        out_specs=pl.BlockSpec((None, b, tn), lambda l, j: (l, 0, j)),
        out_shape=jax.ShapeDtypeStruct((depth, b, cols), jnp.float32),
        compiler_params=pltpu.CompilerParams(vmem_limit_bytes=VMEM_LIMIT),
        name="ada_mod",
    )(c, ada_w, ada_b.reshape(depth, 1, cols))


def _mix_in_kernel(x_ref, mod_ref, nm_ref, win_ref, ws_ref, bst_ref, vn_ref, na_ref,
                   ya_ref, q_ref, k_ref, v_ref):
    gw = GM_HEADS * HEAD_DIM
    x = x_ref[...]
    mod = mod_ref[...]
    h = _rms(x, nm_ref[...]) * (1.0 + mod[1:2, :]) + mod[0:1, :]
    proj = jnp.dot(h.astype(jnp.bfloat16), win_ref[...], preferred_element_type=jnp.float32)
    u = _gelu(proj[:, 0:gw])
    v = _gelu(proj[:, gw:2 * gw])
    q_ref[...] = proj[:, 2 * gw:3 * gw].astype(jnp.bfloat16)
    k_ref[...] = proj[:, 3 * gw:4 * gw].astype(jnp.bfloat16)
    v_ref[...] = proj[:, 4 * gw:5 * gw].astype(jnp.bfloat16)

    r = lax.broadcasted_iota(jnp.int32, (gw, gw), 0) // HEAD_DIM
    cc = lax.broadcasted_iota(jnp.int32, (gw, gw), 1) // HEAD_DIM
    avg = jnp.where(r == cc, 1.0 / HEAD_DIM, 0.0).astype(jnp.bfloat16)
    ms = jnp.dot((v * v).astype(jnp.bfloat16), avg, preferred_element_type=jnp.float32)
    vh = (v * lax.rsqrt(ms + EPS) * vn_ref[...]).astype(jnp.bfloat16)

    t_chunk = lax.broadcasted_iota(jnp.int32, (GM_BLOCK, GM_BLOCK), 0) // CHUNK
    s_chunk = lax.broadcasted_iota(jnp.int32, (GM_BLOCK, GM_BLOCK), 1) // CHUNK
    keep = s_chunk <= t_chunk
    lane = lax.broadcasted_iota(jnp.int32, (GM_BLOCK, LANES), 1)
    first = lane < HEAD_DIM
    bst = bst_ref[...]
    n_blocks = x.shape[0] // GM_BLOCK
    for p in range(GM_HEADS // 2):
        w0 = jnp.where(keep, ws_ref[2 * p], 0.0).astype(jnp.bfloat16)
        w1 = jnp.where(keep, ws_ref[2 * p + 1], 0.0).astype(jnp.bfloat16)
        bias = jnp.where(first, bst[:, 2 * p:2 * p + 1], bst[:, 2 * p + 1:2 * p + 2])
        for nb in range(n_blocks):
            rows = slice(nb * GM_BLOCK, (nb + 1) * GM_BLOCK)
            cols = slice(p * LANES, (p + 1) * LANES)
            vp = vh[rows, cols]
            z0 = jnp.dot(w0, vp, preferred_element_type=jnp.float32)
            z1 = jnp.dot(w1, vp, preferred_element_type=jnp.float32)
            z = jnp.where(first, z0, z1) + bias
            ya_ref[rows, cols] = u[rows, cols] * z
    ya = ya_ref[...]
    ya_ref[...] = _rms(ya, na_ref[...])


def _mix_in(xb, mod_b, norm_mix, w_in_bf, gm_ws, gm_bst, gm_vnorm, out_norm_a):
    seq, d = xb.shape
    gw = GM_HEADS * HEAD_DIM
    ts = min(MIX_TOKENS, seq)
    tok_spec = lambda w: pl.BlockSpec((ts, w), lambda i: (i, 0))
    return pl.pallas_call(
        _mix_in_kernel,
        grid=(seq // ts,),
        in_specs=[tok_spec(d), _full(mod_b), _full(norm_mix), _full(w_in_bf), _full(gm_ws), _full(gm_bst),
                  _full(gm_vnorm), _full(out_norm_a)],
        out_specs=[tok_spec(gw), tok_spec(gw), tok_spec(gw), tok_spec(gw)],
        out_shape=[
            jax.ShapeDtypeStruct((seq, gw), jnp.float32),
            jax.ShapeDtypeStruct((seq, gw), jnp.bfloat16),
            jax.ShapeDtypeStruct((seq, gw), jnp.bfloat16),
            jax.ShapeDtypeStruct((seq, gw), jnp.bfloat16),
        ],
        compiler_params=pltpu.CompilerParams(vmem_limit_bytes=VMEM_LIMIT),
        name="mix_in",
    )(xb, mod_b, norm_mix, w_in_bf, gm_ws, gm_bst, gm_vnorm, out_norm_a)


def _sb_logs(qh, k, mask):
    z = lax.dot_general(qh, k, (((1,), (1,)), ((), ())), preferred_element_type=jnp.float32)
    soft = jnp.log(1.0 + jnp.exp(-jnp.abs(z)))
    log_beta = jnp.minimum(z, 0.0) - soft
    log_keep = log_beta - z
    if mask is not None:
        log_keep = jnp.where(mask, log_keep, 0.0)
    hi = log_keep.astype(jnp.bfloat16)
    lo = (log_keep - hi.astype(jnp.float32)).astype(jnp.bfloat16)
    return log_beta, log_keep, hi, lo


def _sb_later(hi, lo, tri):
    return (jnp.dot(hi, tri, preferred_element_type=jnp.float32)
            + jnp.dot(lo, tri, preferred_element_type=jnp.float32))


def _sb_step(qs, k, v, tri, carries, mask):
    logs = [_sb_logs(qh, k, mask) for qh in qs]
    laters = [_sb_later(lg[2], lg[3], tri) for lg in logs]
    res = []
    for lg, later, carry in zip(logs, laters, carries):
        a = jnp.exp(lg[0] + later + carry)
        if mask is not None:
            a = jnp.where(mask, a, 0.0)
        out = jnp.dot(a.astype(jnp.bfloat16), v, preferred_element_type=jnp.float32)
        res.append((out, carry + later[:, 0:1] + lg[1][:, 0:1]))
    return res


def _sb_kernel(q_ref, k_ref, v_ref, o_ref, acc_ref, car_ref):
    tq, tk = q_ref.shape[0], SB_K
    qi = pl.program_id(1)
    lane = lax.broadcasted_iota(jnp.int32, (tq, LANES), 1)
    first = lane < HEAD_DIM
    q = q_ref[...] * (1.0 / np.sqrt(HEAD_DIM))
    zero = jnp.zeros_like(q)
    qs = (jnp.where(first, q, zero), jnp.where(first, zero, q))
    tri = (lax.broadcasted_iota(jnp.int32, (tk, tk), 0)
           > lax.broadcasted_iota(jnp.int32, (tk, tk), 1)).astype(jnp.bfloat16)
    causal = (lax.broadcasted_iota(jnp.int32, (tq, tk), 1)
              < lax.broadcasted_iota(jnp.int32, (tq, tk), 0))

    start = pl.multiple_of(qi * tk, tk)
    zero_carry = jnp.zeros((tq, 1), jnp.float32)
    res = _sb_step(qs, k_ref[pl.ds(start, tk), :], v_ref[pl.ds(start, tk), :], tri,
                   (zero_carry, zero_carry), causal)
    for hh in range(2):
        acc_ref[hh] = res[hh][0]
        car_ref[hh] = jnp.broadcast_to(res[hh][1], (tq, LANES))

    def body(i, _):
        off = pl.multiple_of((qi - 1 - i) * tk, tk)
        res = _sb_step(qs, k_ref[pl.ds(off, tk), :], v_ref[pl.ds(off, tk), :], tri,
                       (car_ref[0][:, 0:1], car_ref[1][:, 0:1]), None)
        for hh in range(2):
            acc_ref[hh] += res[hh][0]
            car_ref[hh] = jnp.broadcast_to(res[hh][1], (tq, LANES))
        return 0

    lax.fori_loop(0, qi, body, 0)
    o_ref[...] = jnp.where(first, acc_ref[0], acc_ref[1])


def _stick_break(q, k, v):
    seq, w = q.shape
    pairs = w // LANES
    tq = min(SB_Q, seq)
    assert tq == min(SB_K, seq)
    q_spec = pl.BlockSpec((tq, LANES), lambda p, i: (i, p))
    kv_spec = pl.BlockSpec((seq, LANES), lambda p, i: (0, p))
    return pl.pallas_call(
        _sb_kernel,
        grid=(pairs, seq // tq),
        in_specs=[q_spec, kv_spec, kv_spec],
        out_specs=pl.BlockSpec((tq, LANES), lambda p, i: (i, p)),
        out_shape=jax.ShapeDtypeStruct((seq, w), jnp.float32),
        scratch_shapes=[pltpu.VMEM((2, tq, LANES), jnp.float32),
                        pltpu.VMEM((2, tq, LANES), jnp.float32)],
        compiler_params=pltpu.CompilerParams(vmem_limit_bytes=VMEM_LIMIT),
        name="stick_break",
    )(q, k, v)


def _top_rows(problems, k):
    t = problems[0][0].shape[1]
    slot = lax.broadcasted_iota(jnp.int32, (k, t), 0)
    ss = [s for s, _ in problems]
    labels = [label for _, label in problems]
    vals = [jnp.zeros((k, t), jnp.float32) for _ in problems]
    labs = [jnp.zeros((k, t), jnp.float32) for _ in problems]
    for r in range(k):
        ms = [jnp.max(s, axis=0, keepdims=True) for s in ss]
        ls = [jnp.min(jnp.where(s == m, label, jnp.inf), axis=0, keepdims=True)
              for s, m, label in zip(ss, ms, labels)]
        vals = [jnp.where(slot == r, m, v) for m, v in zip(ms, vals)]
        labs = [jnp.where(slot == r, lab, v) for lab, v in zip(ls, labs)]
        ss = [jnp.where(label == lab, -jnp.inf, s) for s, lab, label in zip(ss, ls, labels)]
    return list(zip(vals, labs))


def _pair_candidates(v1, v2):
    t = v1.shape[1]
    sub = lax.broadcasted_iota(jnp.int32, (SUBLANES, t), 0)
    sub_f = sub.astype(jnp.float32)
    sums, poss = [], []
    for i in range(SUBLANES):
        reach = TOPK // (i + 1)
        for j0 in range(0, reach, SUBLANES):
            piece = v1[i:i + 1, :] + v2[j0:j0 + SUBLANES, :]
            if reach - j0 < SUBLANES:
                piece = jnp.where(sub < reach - j0, piece, -jnp.inf)
            sums.append(piece)
            poss.append(sub_f + float(i * TOPK + j0))
    sums.append(v1[SUBLANES:TOPK, :] + v2[0:1, :])
    poss.append((sub_f + float(SUBLANES)) * float(TOPK))
    return jnp.concatenate(sums, axis=0), jnp.concatenate(poss, axis=0)


def _mix_out_kernel(x_ref, ya_ref, yb_ref, mod_ref, nb_ref, wout_ref, nf_ref, wq_ref, k1_ref, k2_ref,
                    x1_ref, h2_ref, eid_ref, gate_ref, q_scr):
    gw = ya_ref.shape[1]
    mod = mod_ref[...]
    ya = ya_ref[...].astype(jnp.bfloat16)
    yb = _rms(yb_ref[...], nb_ref[...]).astype(jnp.bfloat16)
    y = (jnp.dot(ya, wout_ref[0:gw, :], preferred_element_type=jnp.float32)
         + jnp.dot(yb, wout_ref[gw:2 * gw, :], preferred_element_type=jnp.float32))
    x1 = x_ref[...] + mod[2:3, :] * y
    x1_ref[...] = x1
    h2 = _rms(x1, nf_ref[...]) * (1.0 + mod[4:5, :]) + mod[3:4, :]
    h2_ref[...] = h2
    q_scr[...] = jnp.dot(h2.astype(jnp.bfloat16), wq_ref[...], preferred_element_type=jnp.float32)

    k1 = k1_ref[...].astype(jnp.bfloat16)
    k2 = k2_ref[...].astype(jnp.bfloat16)
    t = LANES
    key_f = lax.broadcasted_iota(jnp.int32, (N_KEYS, t), 0).astype(jnp.float32)
    slot = lax.broadcasted_iota(jnp.int32, (TOPK, t), 0)
    slot_f = slot.astype(jnp.float32)
    nt = (((1,), (1,)), ((), ()))

    def sub_key_tops(h, r0):
        c0 = pl.multiple_of(h * 2 * PEER_HALF, 2 * PEER_HALF)
        qa = q_scr[pl.ds(r0, t), pl.ds(c0, PEER_HALF)].astype(jnp.bfloat16)
        qb = q_scr[pl.ds(r0, t), pl.ds(c0 + PEER_HALF, PEER_HALF)].astype(jnp.bfloat16)
        s1 = lax.dot_general(k1, qa, nt, preferred_element_type=jnp.float32)
        s2 = lax.dot_general(k2, qb, nt, preferred_element_type=jnp.float32)
        return _top_rows([(s1, key_f), (s2, key_f)], TOPK)

    def route(i, _):
        heads = (2 * (i % (PEER_HEADS // 2)), 2 * (i % (PEER_HEADS // 2)) + 1)
        r0 = pl.multiple_of((i // (PEER_HEADS // 2)) * t, t)
        halves = [sub_key_tops(h, r0) for h in heads]
        tops = _top_rows([_pair_candidates(hv[0][0], hv[1][0]) for hv in halves], TOPK)
        for h, ((_, i1), (_, i2)), (top_s, pos) in zip(heads, halves, tops):
            eid = jnp.zeros((TOPK, t), jnp.float32)
            for r in range(TOPK):
                pr = pos[r:r + 1, :]
                pi = jnp.floor(pr * (1.0 / TOPK))
                pj = pr - pi * TOPK
                e1 = jnp.sum(jnp.where(slot_f == pi, i1, 0.0), axis=0, keepdims=True)
                e2 = jnp.sum(jnp.where(slot_f == pj, i2, 0.0), axis=0, keepdims=True)
                eid = jnp.where(slot == r, e1 * N_KEYS + e2, eid)
            ex = jnp.exp(top_s - top_s[0:1, :])
            gate = ex / jnp.sum(ex, axis=0, keepdims=True)
            p0 = pl.multiple_of(h * TOPK, TOPK)
            eid_ref[pl.ds(p0, TOPK), pl.ds(r0, t)] = eid.astype(jnp.int32)
            gate_ref[pl.ds(p0, TOPK), pl.ds(r0, t)] = gate
        return 0

    lax.fori_loop(0, (PEER_HEADS // 2) * (x1.shape[0] // t), route, 0)


def _mix_out(xb, ya, yb, mod_b, out_norm_b, w_out_bf, norm_ffn, wq_bf, k1, k2):
    seq, d = xb.shape
    gw = ya.shape[1]
    tt = min(OUT_TOKENS, seq)
    qcols = wq_bf.shape[1]
    tok_spec = lambda w: pl.BlockSpec((tt, w), lambda i: (i, 0))
    pick_spec = pl.BlockSpec((PICKS, tt), lambda i: (0, i))
    return pl.pallas_call(
        _mix_out_kernel,
        grid=(seq // tt,),
        in_specs=[tok_spec(d), tok_spec(gw), tok_spec(gw), _full(mod_b), _full(out_norm_b), _full(w_out_bf),
                  _full(norm_ffn), _full(wq_bf), _full(k1), _full(k2)],
        out_specs=[tok_spec(d), tok_spec(d), pick_spec, pick_spec],
        out_shape=[
            jax.ShapeDtypeStruct((seq, d), jnp.float32),
            jax.ShapeDtypeStruct((seq, d), jnp.float32),
            jax.ShapeDtypeStruct((PICKS, seq), jnp.int32),
            jax.ShapeDtypeStruct((PICKS, seq), jnp.float32),
        ],
        scratch_shapes=[pltpu.VMEM((tt, qcols), jnp.float32)],
        compiler_params=pltpu.CompilerParams(vmem_limit_bytes=VMEM_LIMIT),
        name="mix_out",
    )(xb, ya, yb, mod_b, out_norm_b, w_out_bf, norm_ffn, wq_bf, k1, k2)


def _gather_rows(table, idx):
    n = idx.shape[0]
    width = table.shape[1]
    info = plsc.get_sparse_core_info()
    workers = info.num_cores * info.num_subcores
    per_worker = n // workers
    steps = per_worker // SC_ROWS
    assert per_worker * workers == n and steps * SC_ROWS == per_worker and steps % 2 == 0
    mesh = plsc.VectorSubcoreMesh(core_axis_name="c", subcore_axis_name="s")
    rows_t = pltpu.VMEM((SC_ROWS, width), table.dtype)

    @functools.partial(
        pl.kernel, mesh=mesh,
        out_type=jax.ShapeDtypeStruct((n, width), table.dtype),
        scratch_types=[pltpu.VMEM((steps, SC_ROWS), jnp.int32), rows_t, rows_t,
                       pltpu.SemaphoreType.DMA, pltpu.SemaphoreType.DMA,
                       pltpu.SemaphoreType.DMA, pltpu.SemaphoreType.DMA],
    )
    def gather(table_hbm, idx_hbm, out_hbm, idx_v, rows0, rows1, g0, g1, w0, w1):
        wid = lax.axis_index("s") * info.num_cores + lax.axis_index("c")
        base = wid * per_worker
        pltpu.sync_copy(idx_hbm.at[wid], idx_v)

        def fetch(i, rows, sem):
            return pltpu.make_async_copy(table_hbm.at[idx_v.at[i]], rows, sem)

        def flush(i, rows, sem):
            off = pl.multiple_of(base + i * SC_ROWS, SC_ROWS)
            return pltpu.make_async_copy(rows, out_hbm.at[pl.ds(off, SC_ROWS)], sem)

        fetch(0, rows0, g0).start()

        @pl.loop(0, steps, step=2)
        def _(i):
            @pl.when(i > 0)
            def _():
                flush(i - 1, rows1, w1).wait()
            fetch(i + 1, rows1, g1).start()
            fetch(i, rows0, g0).wait()
            flush(i, rows0, w0).start()
            flush(i, rows0, w0).wait()

            @pl.when(i + 2 < steps)
            def _():
                fetch(i + 2, rows0, g0).start()
            fetch(i + 1, rows1, g1).wait()
            flush(i + 1, rows1, w1).start()

        flush(steps - 1, rows1, w1).wait()

    return gather(table, idx.reshape(workers, steps, SC_ROWS))


def _unpack(words):
    lo = lax.bitcast_convert_type(lax.shift_left(words, 16), jnp.float32)
    hi = lax.bitcast_convert_type(jnp.bitwise_and(words, jnp.int32(-65536)), jnp.float32)
    return lo, hi


def _peer_kernel(x1_ref, h2_ref, gate_ref, u_ref, v_ref, mod_ref, o_ref):
    half = u_ref.shape[1]
    tokens = x1_ref.shape[0]
    gates = gate_ref[...]
    for t in range(tokens):
        rows = slice(t * PICKS, (t + 1) * PICKS)
        h = h2_ref[t:t + 1, :]
        u_lo, u_hi = _unpack(u_ref[rows, :])
        prod = u_lo * h[:, 0:half] + u_hi * h[:, half:2 * half]
        act = _gelu(jnp.sum(prod, axis=1, keepdims=True))
        wgt = gates[:, t:t + 1] * act
        v_lo, v_hi = _unpack(v_ref[rows, :])
        o_ref[t:t + 1, 0:half] = jnp.sum(wgt * v_lo, axis=0, keepdims=True)
        o_ref[t:t + 1, half:2 * half] = jnp.sum(wgt * v_hi, axis=0, keepdims=True)
    o_ref[...] = x1_ref[...] + mod_ref[5:6, :] * o_ref[...]


def _peer_apply(x1, h2, gates, u_rows, v_rows, mod_b):
    seq, d = x1.shape
    half = d // 2
    tp = PEER_TOKENS
    tok_spec = pl.BlockSpec((tp, d), lambda i: (i, 0))
    row_spec = pl.BlockSpec((tp * PICKS, half), lambda i: (i, 0))
    return pl.pallas_call(
        _peer_kernel,
        grid=(seq // tp,),
        in_specs=[tok_spec, tok_spec, pl.BlockSpec((None, PICKS, tp), lambda i: (i, 0, 0)),
                  row_spec, row_spec, _full(mod_b)],
        out_specs=tok_spec,
        out_shape=jax.ShapeDtypeStruct((seq, d), jnp.float32),
        compiler_params=pltpu.CompilerParams(vmem_limit_bytes=VMEM_LIMIT),
        name="peer_apply",
    )(x1, h2, gates, u_rows, v_rows, mod_b)


def _final_kernel(x_ref, g_ref, o_ref):
    o_ref[...] = _rms(x_ref[...], g_ref[...])


def _final_norm(xb, gain):
    seq, d = xb.shape
    tt = min(1024, seq)
    return pl.pallas_call(
        _final_kernel,
        grid=(seq // tt,),
        in_specs=[pl.BlockSpec((tt, d), lambda i: (i, 0)), _full(gain)],
        out_specs=pl.BlockSpec((tt, d), lambda i: (i, 0)),
        out_shape=jax.ShapeDtypeStruct((seq, d), jnp.float32),
        name="final_norm",
    )(xb, gain)


def _pack_table(tab):
    half = tab.shape[1] // 2
    bits = lax.bitcast_convert_type(tab.astype(jnp.bfloat16), jnp.uint16).astype(jnp.uint32)
    words = bits[:, :half] | (bits[:, half:] << 16)
    return lax.bitcast_convert_type(words, jnp.int32)


def kernel(x, c, ada_w, ada_b, norm_mix, norm_ffn, w_in, gm_ws, gm_bs, gm_vnorm, out_norm_a, out_norm_b,
           w_out, peer_wq, peer_k1, peer_k2, peer_u, peer_v, final_norm):
    batch, seq, d = x.shape
    depth = ada_w.shape[0]
    bf = jnp.bfloat16
    mod = _ada_mod(c, ada_w, ada_b).reshape(depth, batch, 6, d)
    xs = [x[b] for b in range(batch)]
    for l in range(depth):
        row = lambda a: a[l].reshape(1, -1)
        w_in_bf, w_out_bf, wq_bf = w_in[l].astype(bf), w_out[l].astype(bf), peer_wq[l].astype(bf)
        gm_bst = gm_bs[l].T
        u_pack = _pack_table(peer_u[l])
        v_pack = _pack_table(peer_v[l])
        for b in range(batch):
            mod_b = mod[l, b]
            ya, q, k, v = _mix_in(xs[b], mod_b, row(norm_mix), w_in_bf, gm_ws[l], gm_bst,
                                  row(gm_vnorm), row(out_norm_a))
            yb = _stick_break(q, k, v)
            x1, h2, eid, gate = _mix_out(xs[b], ya, yb, mod_b, row(out_norm_b), w_out_bf, row(norm_ffn),
                                         wq_bf, peer_k1[l], peer_k2[l])
            idx = eid.T.reshape(seq * PICKS)
            gates = gate.reshape(PICKS, seq // PEER_TOKENS, PEER_TOKENS).transpose(1, 0, 2)
            u_rows = _gather_rows(u_pack, idx)
            v_rows = _gather_rows(v_pack, idx)
            xs[b] = _peer_apply(x1, h2, gates, u_rows, v_rows, mod_b)
    gain = final_norm.reshape(1, d)
    return jnp.stack([_final_norm(xb, gain) for xb in xs], axis=0)
```

```python
import functools

import jax
import jax.numpy as jnp
import numpy as np
from jax import lax
from jax.experimental import pallas as pl
from jax.experimental.pallas import tpu as pltpu
from jax.experimental.pallas import tpu_sc as plsc

EPS = 1e-6
CHUNK = 64
GM_HEADS = 8
GM_BLOCK = 128
HEAD_DIM = 64
PEER_HEADS = 8
PEER_HALF = 128
N_KEYS = 128
TOPK = 16
PICKS = PEER_HEADS * TOPK

LANES = 128
SUBLANES = 8
VMEM_LIMIT = 56 * 1024 * 1024

MIX_TOKENS = 512
SB_Q = 256
SB_K = 256
OUT_TOKENS = 256
SC_ROWS = 64
SC_COLS = 8


def _gelu(x):
    return 0.5 * x * (1.0 + jnp.tanh(np.sqrt(2.0 / np.pi) * (x + 0.044715 * (x * x * x))))


def _rms(x, gain):
    return x * lax.rsqrt(jnp.mean(x * x, axis=-1, keepdims=True) + EPS) * gain


def _full(a):
    return pl.BlockSpec(a.shape, lambda *_: (0,) * a.ndim)


def _mod_kernel(c_ref, w_ref, b_ref, o_ref):
    c = c_ref[...]
    c_act = c * jax.nn.sigmoid(c)
    o_ref[...] = jnp.dot(c_act, w_ref[...], preferred_element_type=jnp.float32) + b_ref[...]


def _ada_mod(c, ada_w, ada_b):
    depth, d, cols = ada_w.shape
    b = c.shape[0]
    tn = cols // 4
    return pl.pallas_call(
        _mod_kernel,
        grid=(depth, cols // tn),
        in_specs=[
            pl.BlockSpec((b, d), lambda l, j: (0, 0)),
            pl.BlockSpec((None, d, tn), lambda l, j: (l, 0, j)),
            pl.BlockSpec((None, 1, tn), lambda l, j: (l, 0, j)),
        ],
        out_specs=pl.BlockSpec((None, b, tn), lambda l, j: (l, 0, j)),
        out_shape=jax.ShapeDtypeStruct((depth, b, cols), jnp.float32),
        compiler_params=pltpu.CompilerParams(vmem_limit_bytes=VMEM_LIMIT),
        name="ada_mod",
    )(c, ada_w, ada_b.reshape(depth, 1, cols))


def _mix_in_kernel(has_peer, *refs):
    if has_peer:
        (x_ref, peer_ref, g2_ref, mod_ref, nm_ref, win_ref, ws_ref, bst_ref, vn_ref, na_ref,
         ya_ref, q_ref, k_ref, v_ref, xo_ref) = refs
        x = x_ref[...] + g2_ref[...] * peer_ref[...]
        xo_ref[...] = x
    else:
        (x_ref, mod_ref, nm_ref, win_ref, ws_ref, bst_ref, vn_ref, na_ref,
         ya_ref, q_ref, k_ref, v_ref) = refs
        x = x_ref[...]
    gw = GM_HEADS * HEAD_DIM
    mod = mod_ref[...]
    h = _rms(x, nm_ref[...]) * (1.0 + mod[1:2, :]) + mod[0:1, :]
    proj = jnp.dot(h.astype(jnp.bfloat16), win_ref[...], preferred_element_type=jnp.float32)
    u = _gelu(proj[:, 0:gw])
    v = _gelu(proj[:, gw:2 * gw])
    q_ref[...] = proj[:, 2 * gw:3 * gw].astype(jnp.bfloat16)
    k_ref[...] = proj[:, 3 * gw:4 * gw].astype(jnp.bfloat16)
    v_ref[...] = proj[:, 4 * gw:5 * gw].astype(jnp.bfloat16)

    r = lax.broadcasted_iota(jnp.int32, (gw, gw), 0) // HEAD_DIM
    cc = lax.broadcasted_iota(jnp.int32, (gw, gw), 1) // HEAD_DIM
    avg = jnp.where(r == cc, 1.0 / HEAD_DIM, 0.0).astype(jnp.bfloat16)
    ms = jnp.dot((v * v).astype(jnp.bfloat16), avg, preferred_element_type=jnp.float32)
    vh = (v * lax.rsqrt(ms + EPS) * vn_ref[...]).astype(jnp.bfloat16)

    t_chunk = lax.broadcasted_iota(jnp.int32, (GM_BLOCK, GM_BLOCK), 0) // CHUNK
    s_chunk = lax.broadcasted_iota(jnp.int32, (GM_BLOCK, GM_BLOCK), 1) // CHUNK
    keep = s_chunk <= t_chunk
    lane = lax.broadcasted_iota(jnp.int32, (GM_BLOCK, LANES), 1)
    first = lane < HEAD_DIM
    bst = bst_ref[...]
    n_blocks = x.shape[0] // GM_BLOCK
    for p in range(GM_HEADS // 2):
        w0 = jnp.where(keep, ws_ref[2 * p], 0.0).astype(jnp.bfloat16)
        w1 = jnp.where(keep, ws_ref[2 * p + 1], 0.0).astype(jnp.bfloat16)
        bias = jnp.where(first, bst[:, 2 * p:2 * p + 1], bst[:, 2 * p + 1:2 * p + 2])
        for nb in range(n_blocks):
            rows = slice(nb * GM_BLOCK, (nb + 1) * GM_BLOCK)
            cols = slice(p * LANES, (p + 1) * LANES)
            vp = vh[rows, cols]
            z0 = jnp.dot(w0, vp, preferred_element_type=jnp.float32)
            z1 = jnp.dot(w1, vp, preferred_element_type=jnp.float32)
            z = jnp.where(first, z0, z1) + bias
            ya_ref[rows, cols] = u[rows, cols] * z
    ya = ya_ref[...]
    ya_ref[...] = _rms(ya, na_ref[...])


def _mix_in(xb, peer, gate2, mod_b, norm_mix, w_in_bf, gm_ws, gm_bst, gm_vnorm, out_norm_a):
    seq, d = xb.shape
    gw = GM_HEADS * HEAD_DIM
    ts = min(MIX_TOKENS, seq)
    tok_spec = lambda w: pl.BlockSpec((ts, w), lambda i: (i, 0))
    has_peer = peer is not None
    weights = (mod_b, norm_mix, w_in_bf, gm_ws, gm_bst, gm_vnorm, out_norm_a)
    acts = (xb, peer, gate2) if has_peer else (xb,)
    act_specs = [tok_spec(d), tok_spec(d), _full(gate2)] if has_peer else [tok_spec(d)]
    out_specs = [tok_spec(gw)] * 4 + ([tok_spec(d)] if has_peer else [])
    out_shape = ([jax.ShapeDtypeStruct((seq, gw), jnp.float32)]
                 + [jax.ShapeDtypeStruct((seq, gw), jnp.bfloat16)] * 3
                 + ([jax.ShapeDtypeStruct((seq, d), jnp.float32)] if has_peer else []))
    outs = pl.pallas_call(
        functools.partial(_mix_in_kernel, has_peer),
        grid=(seq // ts,),
        in_specs=act_specs + [_full(w) for w in weights],
        out_specs=out_specs,
        out_shape=out_shape,
        compiler_params=pltpu.CompilerParams(vmem_limit_bytes=VMEM_LIMIT),
        name="mix_in",
    )(*acts, *weights)
    return tuple(outs) if has_peer else (*outs, xb)


def _sb_logs(qh, k, mask):
    z = lax.dot_general(qh, k, (((1,), (1,)), ((), ())), preferred_element_type=jnp.float32)
    soft = jnp.log(1.0 + jnp.exp(-jnp.abs(z)))
    log_beta = jnp.minimum(z, 0.0) - soft
    log_keep = log_beta - z
    if mask is not None:
        log_keep = jnp.where(mask, log_keep, 0.0)
    hi = log_keep.astype(jnp.bfloat16)
    lo = (log_keep - hi.astype(jnp.float32)).astype(jnp.bfloat16)
    return log_beta, log_keep, hi, lo


def _sb_later(hi, lo, tri):
    return (jnp.dot(hi, tri, preferred_element_type=jnp.float32)
            + jnp.dot(lo, tri, preferred_element_type=jnp.float32))


def _sb_step(qs, k, v, tri, carries, mask):
    logs = [_sb_logs(qh, k, mask) for qh in qs]
    laters = [_sb_later(lg[2], lg[3], tri) for lg in logs]
    res = []
    for lg, later, carry in zip(logs, laters, carries):
        a = jnp.exp(lg[0] + later + carry)
        if mask is not None:
            a = jnp.where(mask, a, 0.0)
        out = jnp.dot(a.astype(jnp.bfloat16), v, preferred_element_type=jnp.float32)
        res.append((out, carry + later[:, 0:1] + lg[1][:, 0:1]))
    return res


def _sb_kernel(q_ref, k_ref, v_ref, o_ref, acc_ref, car_ref):
    tq, tk = q_ref.shape[0], SB_K
    qi = pl.program_id(1)
    lane = lax.broadcasted_iota(jnp.int32, (tq, LANES), 1)
    first = lane < HEAD_DIM
    q = q_ref[...] * (1.0 / np.sqrt(HEAD_DIM))
    zero = jnp.zeros_like(q)
    qs = (jnp.where(first, q, zero), jnp.where(first, zero, q))
    tri = (lax.broadcasted_iota(jnp.int32, (tk, tk), 0)
           > lax.broadcasted_iota(jnp.int32, (tk, tk), 1)).astype(jnp.bfloat16)
    causal = (lax.broadcasted_iota(jnp.int32, (tq, tk), 1)
              < lax.broadcasted_iota(jnp.int32, (tq, tk), 0))

    start = pl.multiple_of(qi * tk, tk)
    zero_carry = jnp.zeros((tq, 1), jnp.float32)
    res = _sb_step(qs, k_ref[pl.ds(start, tk), :], v_ref[pl.ds(start, tk), :], tri,
                   (zero_carry, zero_carry), causal)
    for hh in range(2):
        acc_ref[hh] = res[hh][0]
        car_ref[hh] = jnp.broadcast_to(res[hh][1], (tq, LANES))

    def body(i, _):
        off = pl.multiple_of((qi - 1 - i) * tk, tk)
        res = _sb_step(qs, k_ref[pl.ds(off, tk), :], v_ref[pl.ds(off, tk), :], tri,
                       (car_ref[0][:, 0:1], car_ref[1][:, 0:1]), None)
        for hh in range(2):
            acc_ref[hh] += res[hh][0]
            car_ref[hh] = jnp.broadcast_to(res[hh][1], (tq, LANES))
        return 0

    lax.fori_loop(0, qi, body, 0)
    o_ref[...] = jnp.where(first, acc_ref[0], acc_ref[1])


def _stick_break(q, k, v):
    seq, w = q.shape
    pairs = w // LANES
    tq = min(SB_Q, seq)
    assert tq == min(SB_K, seq)
    q_spec = pl.BlockSpec((tq, LANES), lambda p, i: (i, p))
    kv_spec = pl.BlockSpec((seq, LANES), lambda p, i: (0, p))
    return pl.pallas_call(
        _sb_kernel,
        grid=(pairs, seq // tq),
        in_specs=[q_spec, kv_spec, kv_spec],
        out_specs=pl.BlockSpec((tq, LANES), lambda p, i: (i, p)),
        out_shape=jax.ShapeDtypeStruct((seq, w), jnp.float32),
        scratch_shapes=[pltpu.VMEM((2, tq, LANES), jnp.float32),
                        pltpu.VMEM((2, tq, LANES), jnp.float32)],
        compiler_params=pltpu.CompilerParams(vmem_limit_bytes=VMEM_LIMIT),
        name="stick_break",
    )(q, k, v)


def _top_rows(problems, k):
    t = problems[0][0].shape[1]
    slot = lax.broadcasted_iota(jnp.int32, (k, t), 0)
    ss = [s for s, _ in problems]
    labels = [label for _, label in problems]
    vals = [jnp.zeros((k, t), jnp.float32) for _ in problems]
    labs = [jnp.zeros((k, t), jnp.float32) for _ in problems]
    for r in range(k):
        ms = [jnp.max(s, axis=0, keepdims=True) for s in ss]
        ls = [jnp.min(jnp.where(s == m, label, jnp.inf), axis=0, keepdims=True)
              for s, m, label in zip(ss, ms, labels)]
        vals = [jnp.where(slot == r, m, v) for m, v in zip(ms, vals)]
        labs = [jnp.where(slot == r, lab, v) for lab, v in zip(ls, labs)]
        ss = [jnp.where(label == lab, -jnp.inf, s) for s, lab, label in zip(ss, ls, labels)]
    return list(zip(vals, labs))


def _pair_candidates(v1, v2):
    t = v1.shape[1]
    sub = lax.broadcasted_iota(jnp.int32, (SUBLANES, t), 0)
    sub_f = sub.astype(jnp.float32)
    sums, poss = [], []
    for i in range(SUBLANES):
        reach = TOPK // (i + 1)
        for j0 in range(0, reach, SUBLANES):
            piece = v1[i:i + 1, :] + v2[j0:j0 + SUBLANES, :]
            if reach - j0 < SUBLANES:
                piece = jnp.where(sub < reach - j0, piece, -jnp.inf)
            sums.append(piece)
            poss.append(sub_f + float(i * TOPK + j0))
    sums.append(v1[SUBLANES:TOPK, :] + v2[0:1, :])
    poss.append((sub_f + float(SUBLANES)) * float(TOPK))
    return jnp.concatenate(sums, axis=0), jnp.concatenate(poss, axis=0)


def _mix_out_kernel(x_ref, ya_ref, yb_ref, mod_ref, nb_ref, wout_ref, nf_ref, wq_ref, k1_ref, k2_ref,
                    x1_ref, h2_ref, eid_ref, gate_ref, q_scr, eid_scr, gate_scr):
    gw = ya_ref.shape[1]
    mod = mod_ref[...]
    ya = ya_ref[...].astype(jnp.bfloat16)
    yb = _rms(yb_ref[...], nb_ref[...]).astype(jnp.bfloat16)
    y = (jnp.dot(ya, wout_ref[0:gw, :], preferred_element_type=jnp.float32)
         + jnp.dot(yb, wout_ref[gw:2 * gw, :], preferred_element_type=jnp.float32))
    x1 = x_ref[...] + mod[2:3, :] * y
    x1_ref[...] = x1
    h2 = _rms(x1, nf_ref[...]) * (1.0 + mod[4:5, :]) + mod[3:4, :]
    h2_ref[...] = h2
    q_scr[...] = jnp.dot(h2.astype(jnp.bfloat16), wq_ref[...], preferred_element_type=jnp.float32)

    k1 = k1_ref[...].astype(jnp.bfloat16)
    k2 = k2_ref[...].astype(jnp.bfloat16)
    t = LANES
    key_f = lax.broadcasted_iota(jnp.int32, (N_KEYS, t), 0).astype(jnp.float32)
    slot = lax.broadcasted_iota(jnp.int32, (TOPK, t), 0)
    slot_f = slot.astype(jnp.float32)
    nt = (((1,), (1,)), ((), ()))

    def sub_key_tops(h, r0):
        c0 = pl.multiple_of(h * 2 * PEER_HALF, 2 * PEER_HALF)
        qa = q_scr[pl.ds(r0, t), pl.ds(c0, PEER_HALF)].astype(jnp.bfloat16)
        qb = q_scr[pl.ds(r0, t), pl.ds(c0 + PEER_HALF, PEER_HALF)].astype(jnp.bfloat16)
        s1 = lax.dot_general(k1, qa, nt, preferred_element_type=jnp.float32)
        s2 = lax.dot_general(k2, qb, nt, preferred_element_type=jnp.float32)
        return _top_rows([(s1, key_f), (s2, key_f)], TOPK)

    def route(i, _):
        heads = (2 * (i % (PEER_HEADS // 2)), 2 * (i % (PEER_HEADS // 2)) + 1)
        r0 = pl.multiple_of((i // (PEER_HEADS // 2)) * t, t)
        halves = [sub_key_tops(h, r0) for h in heads]
        tops = _top_rows([_pair_candidates(hv[0][0], hv[1][0]) for hv in halves], TOPK)
        for h, ((_, i1), (_, i2)), (top_s, pos) in zip(heads, halves, tops):
            eid = jnp.zeros((TOPK, t), jnp.float32)
            for r in range(TOPK):
                pr = pos[r:r + 1, :]
                pi = jnp.floor(pr * (1.0 / TOPK))
                pj = pr - pi * TOPK
                e1 = jnp.sum(jnp.where(slot_f == pi, i1, 0.0), axis=0, keepdims=True)
                e2 = jnp.sum(jnp.where(slot_f == pj, i2, 0.0), axis=0, keepdims=True)
                eid = jnp.where(slot == r, e1 * N_KEYS + e2, eid)
            ex = jnp.exp(top_s - top_s[0:1, :])
            gate = ex / jnp.sum(ex, axis=0, keepdims=True)
            p0 = pl.multiple_of(h * TOPK, TOPK)
            eid_scr[pl.ds(p0, TOPK), pl.ds(r0, t)] = eid
            gate_scr[pl.ds(p0, TOPK), pl.ds(r0, t)] = gate
        return 0

    lax.fori_loop(0, (PEER_HEADS // 2) * (x1.shape[0] // t), route, 0)
    eid_ref[...] = eid_scr[...].T.astype(jnp.int32)
    gate_ref[...] = gate_scr[...].T


def _mix_out(xb, ya, yb, mod_b, out_norm_b, w_out_bf, norm_ffn, wq_bf, k1, k2):
    seq, d = xb.shape
    gw = ya.shape[1]
    tt = min(OUT_TOKENS, seq)
    qcols = wq_bf.shape[1]
    tok_spec = lambda w: pl.BlockSpec((tt, w), lambda i: (i, 0))
    return pl.pallas_call(
        _mix_out_kernel,
        grid=(seq // tt,),
        in_specs=[tok_spec(d), tok_spec(gw), tok_spec(gw), _full(mod_b), _full(out_norm_b), _full(w_out_bf),
                  _full(norm_ffn), _full(wq_bf), _full(k1), _full(k2)],
        out_specs=[tok_spec(d), tok_spec(d), tok_spec(PICKS), tok_spec(PICKS)],
        out_shape=[
            jax.ShapeDtypeStruct((seq, d), jnp.float32),
            jax.ShapeDtypeStruct((seq, d), jnp.float32),
            jax.ShapeDtypeStruct((seq, PICKS), jnp.int32),
            jax.ShapeDtypeStruct((seq, PICKS), jnp.float32),
        ],
        scratch_shapes=[pltpu.VMEM((tt, qcols), jnp.float32), pltpu.VMEM((PICKS, tt), jnp.float32),
                        pltpu.VMEM((PICKS, tt), jnp.float32)],
        compiler_params=pltpu.CompilerParams(vmem_limit_bytes=VMEM_LIMIT),
        name="mix_out",
    )(xb, ya, yb, mod_b, out_norm_b, w_out_bf, norm_ffn, wq_bf, k1, k2)


def _unpack_words(words):
    lo = plsc.bitcast(lax.shift_left(words, 16), jnp.float32)
    hi = plsc.bitcast(jnp.bitwise_and(words, jnp.int32(-65536)), jnp.float32)
    return lo, hi


def _peer_experts(u_tab, v_tab, idx, gates, h2):
    seq, d = h2.shape
    width = u_tab.shape[1]
    info = plsc.get_sparse_core_info()
    lanes = info.num_lanes
    workers = info.num_cores * info.num_subcores
    tokens = seq // workers
    steps = 2 * tokens
    assert tokens * workers == seq and tokens % 2 == 0 and PICKS == 2 * SC_ROWS and d == 2 * width
    kchunks = width // lanes
    mesh = plsc.VectorSubcoreMesh(core_axis_name="c", subcore_axis_name="s")
    rows_t = pltpu.VMEM((SC_ROWS, width), jnp.int32)
    vec = lambda m: pltpu.VMEM((m,), jnp.float32)
    sem = pltpu.SemaphoreType.DMA
    c_gelu = float(np.sqrt(2.0 / np.pi))

    @functools.partial(
        pl.kernel, mesh=mesh,
        compiler_params=pltpu.CompilerParams(needs_layout_passes=False),
        out_type=jax.ShapeDtypeStruct((seq, d), jnp.float32),
        scratch_types=[pltpu.VMEM((steps, SC_ROWS), jnp.int32), rows_t, rows_t,
                       vec(d), vec(d), vec(PICKS), vec(PICKS), vec(d), vec(d), vec(PICKS), vec(PICKS)]
                      + [sem] * 8,
    )
    def experts(u_hbm, v_hbm, idx_hbm, gate_hbm, h2_hbm, out_hbm, idx_v, rows0, rows1,
                x0, x1, gt0, gt1, o0, o1, act_v, w_v, g0, g1, sx0, sx1, sg0, sg1, so0, so1):
        wid = lax.axis_index("s") * info.num_cores + lax.axis_index("c")
        tok0 = wid * tokens
        pltpu.sync_copy(idx_hbm.at[wid], idx_v)
        lane = lax.iota(jnp.int32, lanes)
        xs, gts, outs = (x0, x1), (gt0, gt1), (o0, o1)
        sxs, sgs, sos = (sx0, sx1), (sg0, sg1), (so0, so1)

        def fetch(tab, i, rows, s):
            return pltpu.make_async_copy(tab.at[idx_v.at[i]], rows, s)

        def xcopy(tk, par):
            return pltpu.make_async_copy(h2_hbm.at[tok0 + tk], xs[par], sxs[par])

        def gcopy(tk, par):
            return pltpu.make_async_copy(gate_hbm.at[tok0 + tk], gts[par], sgs[par])

        def ocopy(tk, par):
            return pltpu.make_async_copy(outs[par], out_hbm.at[tok0 + tk], sos[par])

        def dots(rows, xv, aoff):
            @pl.loop(0, SC_ROWS, step=lanes)
            def _(g):
                res = jnp.zeros((lanes,), jnp.float32)
                for r4 in range(0, lanes, 4):
                    def body(kk, accs):
                        off = pl.multiple_of(kk * lanes, lanes)
                        xl = xv[pl.ds(off, lanes)]
                        xh = xv[pl.ds(width + off, lanes)]
                        new = []
                        for r in range(4):
                            lo, hi = _unpack_words(rows[g + r4 + r, pl.ds(off, lanes)])
                            new.append(accs[r] + lo * xl + hi * xh)
                        return tuple(new)
                    zeros = tuple(jnp.zeros((lanes,), jnp.float32) for _ in range(4))
                    accs = lax.fori_loop(0, kchunks, body, zeros)
                    for r in range(4):
                        res = jnp.where(lane == r4 + r, jnp.sum(accs[r]), res)
                act_v[pl.ds(pl.multiple_of(aoff + g, lanes), lanes)] = res

        def weights(gv):
            for c in range(PICKS // lanes):
                a = act_v[pl.ds(c * lanes, lanes)]
                y = c_gelu * (a + 0.044715 * (a * a * a))
                th = 1.0 - 2.0 / (1.0 + jnp.exp(2.0 * y))
                w_v[pl.ds(c * lanes, lanes)] = gv[pl.ds(c * lanes, lanes)] * (0.5 * a * (1.0 + th))

        def combine(rows, woff, ov, first):
            for cc in range(0, kchunks, SC_COLS):
                def body(j, accs):
                    wj = plsc.load_gather(w_v, [jnp.full((lanes,), woff, jnp.int32) + j])
                    new = []
                    for c in range(SC_COLS):
                        lo, hi = _unpack_words(rows[j, pl.ds((cc + c) * lanes, lanes)])
                        new.append(accs[2 * c] + wj * lo)
                        new.append(accs[2 * c + 1] + wj * hi)
                    return tuple(new)
                zeros = tuple(jnp.zeros((lanes,), jnp.float32) for _ in range(2 * SC_COLS))
                accs = lax.fori_loop(0, SC_ROWS, body, zeros)
                for c in range(SC_COLS):
                    lo_sl = pl.ds((cc + c) * lanes, lanes)
                    hi_sl = pl.ds(width + (cc + c) * lanes, lanes)
                    if first:
                        ov[lo_sl] = accs[2 * c]
                        ov[hi_sl] = accs[2 * c + 1]
                    else:
                        ov[lo_sl] = ov[lo_sl] + accs[2 * c]
                        ov[hi_sl] = ov[hi_sl] + accs[2 * c + 1]

        def token(tk, par):
            xv, gv, ov = xs[par], gts[par], outs[par]
            fetch(u_hbm, 2 * tk + 1, rows1, g1).start()
            fetch(u_hbm, 2 * tk, rows0, g0).wait()
            xcopy(tk, par).wait()
            dots(rows0, xv, 0)
            fetch(v_hbm, 2 * tk, rows0, g0).start()
            fetch(u_hbm, 2 * tk + 1, rows1, g1).wait()
            dots(rows1, xv, SC_ROWS)
            gcopy(tk, par).wait()
            weights(gv)
            fetch(v_hbm, 2 * tk + 1, rows1, g1).start()
            fetch(v_hbm, 2 * tk, rows0, g0).wait()

            @pl.when(tk >= 2)
            def _():
                ocopy(tk - 2, par).wait()
            combine(rows0, 0, ov, True)

            @pl.when(tk + 1 < tokens)
            def _():
                fetch(u_hbm, 2 * tk + 2, rows0, g0).start()
                xcopy(tk + 1, 1 - par).start()
                gcopy(tk + 1, 1 - par).start()
            fetch(v_hbm, 2 * tk + 1, rows1, g1).wait()
            combine(rows1, SC_ROWS, ov, False)
            ocopy(tk, par).start()

        fetch(u_hbm, 0, rows0, g0).start()
        xcopy(0, 0).start()
        gcopy(0, 0).start()

        @pl.loop(0, tokens, step=2)
        def _(tk):
            token(tk, 0)
            token(tk + 1, 1)

        ocopy(tokens - 2, 0).wait()
        ocopy(tokens - 1, 1).wait()

    return experts(u_tab, v_tab, idx.reshape(workers, steps, SC_ROWS), gates, h2)


def _final_kernel(x_ref, peer_ref, g2_ref, g_ref, o_ref):
    o_ref[...] = _rms(x_ref[...] + g2_ref[...] * peer_ref[...], g_ref[...])


def _final_norm(x1, peer, gate2, gain):
    seq, d = x1.shape
    tt = min(1024, seq)
    tok_spec = pl.BlockSpec((tt, d), lambda i: (i, 0))
    return pl.pallas_call(
        _final_kernel,
        grid=(seq // tt,),
        in_specs=[tok_spec, tok_spec, _full(gate2), _full(gain)],
        out_specs=tok_spec,
        out_shape=jax.ShapeDtypeStruct((seq, d), jnp.float32),
        name="final_norm",
    )(x1, peer, gate2, gain)


def _pack_table(tab):
    half = tab.shape[1] // 2
    bits = lax.bitcast_convert_type(tab.astype(jnp.bfloat16), jnp.uint16).astype(jnp.uint32)
    words = bits[:, :half] | (bits[:, half:] << 16)
    return lax.bitcast_convert_type(words, jnp.int32)


def kernel(x, c, ada_w, ada_b, norm_mix, norm_ffn, w_in, gm_ws, gm_bs, gm_vnorm, out_norm_a, out_norm_b,
           w_out, peer_wq, peer_k1, peer_k2, peer_u, peer_v, final_norm):
    batch, seq, d = x.shape
    depth = ada_w.shape[0]
    bf = jnp.bfloat16
    mod = _ada_mod(c, ada_w, ada_b).reshape(depth, batch, 6, d)
    state = [(x[b], None, None) for b in range(batch)]
    for l in range(depth):
        row = lambda a: a[l].reshape(1, -1)
        w_in_bf, w_out_bf, wq_bf = w_in[l].astype(bf), w_out[l].astype(bf), peer_wq[l].astype(bf)
        gm_bst = gm_bs[l].T
        u_pack = _pack_table(peer_u[l])
        v_pack = _pack_table(peer_v[l])
        for b in range(batch):
            mod_b = mod[l, b]
            ya, q, k, v, xb = _mix_in(*state[b], mod_b, row(norm_mix), w_in_bf, gm_ws[l], gm_bst,
                                      row(gm_vnorm), row(out_norm_a))
            yb = _stick_break(q, k, v)
            x1, h2, eid, gate = _mix_out(xb, ya, yb, mod_b, row(out_norm_b), w_out_bf, row(norm_ffn),
                                         wq_bf, peer_k1[l], peer_k2[l])
            state[b] = (x1, _peer_experts(u_pack, v_pack, eid, gate, h2), mod_b[5:6])
    gain = final_norm.reshape(1, d)
    return jnp.stack([_final_norm(*st, gain) for st in state], axis=0)
```

```python
import functools

import jax
import jax.numpy as jnp
import numpy as np
from jax import lax
from jax.experimental import pallas as pl
from jax.experimental.pallas import tpu as pltpu
from jax.experimental.pallas import tpu_sc as plsc

EPS = 1e-6
CHUNK = 64
GM_HEADS = 8
GM_BLOCK = 128
HEAD_DIM = 64
PEER_HEADS = 8
PEER_HALF = 128
N_KEYS = 128
TOPK = 16
PICKS = PEER_HEADS * TOPK

LANES = 128
SUBLANES = 8
VMEM_LIMIT = 56 * 1024 * 1024

MIX_TOKENS = 512
SB_Q = 256
SB_K = 256
OUT_TOKENS = 256
SC_ROWS = 64
SC_COLS = 8
SC_DOT_ROWS = 8


def _gelu(x):
    return 0.5 * x * (1.0 + jnp.tanh(np.sqrt(2.0 / np.pi) * (x + 0.044715 * (x * x * x))))


def _rms(x, gain):
    return x * lax.rsqrt(jnp.mean(x * x, axis=-1, keepdims=True) + EPS) * gain


def _full(a):
    return pl.BlockSpec(a.shape, lambda *_: (0,) * a.ndim)


def _mod_kernel(c_ref, w_ref, b_ref, o_ref):
    c = c_ref[...]
    c_act = c * jax.nn.sigmoid(c)
    o_ref[...] = jnp.dot(c_act, w_ref[...], preferred_element_type=jnp.float32) + b_ref[...]


def _ada_mod(c, ada_w, ada_b):
    depth, d, cols = ada_w.shape
    b = c.shape[0]
    tn = cols // 4
    return pl.pallas_call(
        _mod_kernel,
        grid=(depth, cols // tn),
        in_specs=[
            pl.BlockSpec((b, d), lambda l, j: (0, 0)),
            pl.BlockSpec((None, d, tn), lambda l, j: (l, 0, j)),
            pl.BlockSpec((None, 1, tn), lambda l, j: (l, 0, j)),
        ],
        out_specs=pl.BlockSpec((None, b, tn), lambda l, j: (l, 0, j)),
        out_shape=jax.ShapeDtypeStruct((depth, b, cols), jnp.float32),
        compiler_params=pltpu.CompilerParams(vmem_limit_bytes=VMEM_LIMIT),
        name="ada_mod",
    )(c, ada_w, ada_b.reshape(depth, 1, cols))


def _mix_in_kernel(has_peer, *refs):
    if has_peer:
        (x_ref, peer_ref, g2_ref, mod_ref, nm_ref, win_ref, ws_ref, bst_ref, vn_ref, na_ref,
         ya_ref, q_ref, k_ref, v_ref, xo_ref) = refs
        x = x_ref[...] + g2_ref[...] * peer_ref[...]
        xo_ref[...] = x
    else:
        (x_ref, mod_ref, nm_ref, win_ref, ws_ref, bst_ref, vn_ref, na_ref,
         ya_ref, q_ref, k_ref, v_ref) = refs
        x = x_ref[...]
    gw = GM_HEADS * HEAD_DIM
    mod = mod_ref[...]
    h = _rms(x, nm_ref[...]) * (1.0 + mod[1:2, :]) + mod[0:1, :]
    proj = jnp.dot(h.astype(jnp.bfloat16), win_ref[...], preferred_element_type=jnp.float32)
    u = _gelu(proj[:, 0:gw])
    v = _gelu(proj[:, gw:2 * gw])
    q_ref[...] = proj[:, 2 * gw:3 * gw].astype(jnp.bfloat16)
    k_ref[...] = proj[:, 3 * gw:4 * gw].astype(jnp.bfloat16)
    v_ref[...] = proj[:, 4 * gw:5 * gw].astype(jnp.bfloat16)

    r = lax.broadcasted_iota(jnp.int32, (gw, gw), 0) // HEAD_DIM
    cc = lax.broadcasted_iota(jnp.int32, (gw, gw), 1) // HEAD_DIM
    avg = jnp.where(r == cc, 1.0 / HEAD_DIM, 0.0).astype(jnp.bfloat16)
    ms = jnp.dot((v * v).astype(jnp.bfloat16), avg, preferred_element_type=jnp.float32)
    vh = (v * lax.rsqrt(ms + EPS) * vn_ref[...]).astype(jnp.bfloat16)

    t_chunk = lax.broadcasted_iota(jnp.int32, (GM_BLOCK, GM_BLOCK), 0) // CHUNK
    s_chunk = lax.broadcasted_iota(jnp.int32, (GM_BLOCK, GM_BLOCK), 1) // CHUNK
    keep = s_chunk <= t_chunk
    lane = lax.broadcasted_iota(jnp.int32, (GM_BLOCK, LANES), 1)
    first = lane < HEAD_DIM
    bst = bst_ref[...]
    n_blocks = x.shape[0] // GM_BLOCK
    for p in range(GM_HEADS // 2):
        w0 = jnp.where(keep, ws_ref[2 * p], 0.0).astype(jnp.bfloat16)
        w1 = jnp.where(keep, ws_ref[2 * p + 1], 0.0).astype(jnp.bfloat16)
        bias = jnp.where(first, bst[:, 2 * p:2 * p + 1], bst[:, 2 * p + 1:2 * p + 2])
        for nb in range(n_blocks):
            rows = slice(nb * GM_BLOCK, (nb + 1) * GM_BLOCK)
            cols = slice(p * LANES, (p + 1) * LANES)
            vp = vh[rows, cols]
            z0 = jnp.dot(w0, vp, preferred_element_type=jnp.float32)
            z1 = jnp.dot(w1, vp, preferred_element_type=jnp.float32)
            z = jnp.where(first, z0, z1) + bias
            ya_ref[rows, cols] = u[rows, cols] * z
    ya = ya_ref[...]
    ya_ref[...] = _rms(ya, na_ref[...])


def _mix_in(xb, peer, gate2, mod_b, norm_mix, w_in_bf, gm_ws, gm_bst, gm_vnorm, out_norm_a):
    seq, d = xb.shape
    gw = GM_HEADS * HEAD_DIM
    ts = min(MIX_TOKENS, seq)
    tok_spec = lambda w: pl.BlockSpec((ts, w), lambda i: (i, 0))
    has_peer = peer is not None
    weights = (mod_b, norm_mix, w_in_bf, gm_ws, gm_bst, gm_vnorm, out_norm_a)
    acts = (xb, peer, gate2) if has_peer else (xb,)
    act_specs = [tok_spec(d), tok_spec(d), _full(gate2)] if has_peer else [tok_spec(d)]
    out_specs = [tok_spec(gw)] * 4 + ([tok_spec(d)] if has_peer else [])
    out_shape = ([jax.ShapeDtypeStruct((seq, gw), jnp.float32)]
                 + [jax.ShapeDtypeStruct((seq, gw), jnp.bfloat16)] * 3
                 + ([jax.ShapeDtypeStruct((seq, d), jnp.float32)] if has_peer else []))
    outs = pl.pallas_call(
        functools.partial(_mix_in_kernel, has_peer),
        grid=(seq // ts,),
        in_specs=act_specs + [_full(w) for w in weights],
        out_specs=out_specs,
        out_shape=out_shape,
        compiler_params=pltpu.CompilerParams(vmem_limit_bytes=VMEM_LIMIT),
        name="mix_in",
    )(*acts, *weights)
    return tuple(outs) if has_peer else (*outs, xb)


def _sb_logs(qh, k, mask):
    z = lax.dot_general(qh, k, (((1,), (1,)), ((), ())), preferred_element_type=jnp.float32)
    soft = jnp.log(1.0 + jnp.exp(-jnp.abs(z)))
    log_beta = jnp.minimum(z, 0.0) - soft
    log_keep = log_beta - z
    if mask is not None:
        log_keep = jnp.where(mask, log_keep, 0.0)
    hi = log_keep.astype(jnp.bfloat16)
    lo = (log_keep - hi.astype(jnp.float32)).astype(jnp.bfloat16)
    return log_beta, log_keep, hi, lo


def _sb_later(hi, lo, tri):
    return (jnp.dot(hi, tri, preferred_element_type=jnp.float32)
            + jnp.dot(lo, tri, preferred_element_type=jnp.float32))


def _sb_step(qs, kvs, tri, carries, mask):
    logs = [[_sb_logs(qh, k, mask) for qh in qs] for k, _ in kvs]
    laters = [[_sb_later(lg[2], lg[3], tri) for lg in blk] for blk in logs]
    res = []
    for hh, carry in enumerate(carries):
        out = None
        for (_, v), blk, lat in zip(kvs, logs, laters):
            lg, later = blk[hh], lat[hh]
            a = jnp.exp(lg[0] + later + carry)
            if mask is not None:
                a = jnp.where(mask, a, 0.0)
            part = jnp.dot(a.astype(jnp.bfloat16), v, preferred_element_type=jnp.float32)
            out = part if out is None else out + part
            carry = carry + later[:, 0:1] + lg[1][:, 0:1]
        res.append((out, carry))
    return res


def _sb_kernel(q_ref, k_ref, v_ref, o_ref, acc_ref, car_ref):
    tq, tk = q_ref.shape[0], SB_K
    qi = pl.program_id(1)
    lane = lax.broadcasted_iota(jnp.int32, (tq, LANES), 1)
    first = lane < HEAD_DIM
    q = q_ref[...] * (1.0 / np.sqrt(HEAD_DIM))
    zero = jnp.zeros_like(q)
    qs = (jnp.where(first, q, zero), jnp.where(first, zero, q))
    tri = (lax.broadcasted_iota(jnp.int32, (tk, tk), 0)
           > lax.broadcasted_iota(jnp.int32, (tk, tk), 1)).astype(jnp.bfloat16)
    causal = (lax.broadcasted_iota(jnp.int32, (tq, tk), 1)
              < lax.broadcasted_iota(jnp.int32, (tq, tk), 0))

    def kv(block):
        off = pl.multiple_of(block * tk, tk)
        return k_ref[pl.ds(off, tk), :], v_ref[pl.ds(off, tk), :]

    def accumulate(res, assign=False):
        for hh in range(2):
            acc_ref[hh] = res[hh][0] if assign else acc_ref[hh] + res[hh][0]
            car_ref[hh] = jnp.broadcast_to(res[hh][1], (tq, LANES))

    def carries():
        return car_ref[0][:, 0:1], car_ref[1][:, 0:1]

    zero_carry = jnp.zeros((tq, 1), jnp.float32)
    accumulate(_sb_step(qs, [kv(qi)], tri, (zero_carry, zero_carry), causal), assign=True)

    def body(i, _):
        accumulate(_sb_step(qs, [kv(qi - 1 - 2 * i), kv(qi - 2 - 2 * i)], tri, carries(), None))
        return 0

    lax.fori_loop(0, qi // 2, body, 0)

    @pl.when(qi % 2 == 1)
    def _():
        accumulate(_sb_step(qs, [kv(0)], tri, carries(), None))

    o_ref[...] = jnp.where(first, acc_ref[0], acc_ref[1])


def _stick_break(q, k, v):
    seq, w = q.shape
    pairs = w // LANES
    tq = min(SB_Q, seq)
    assert tq == min(SB_K, seq)
    q_spec = pl.BlockSpec((tq, LANES), lambda p, i: (i, p))
    kv_spec = pl.BlockSpec((seq, LANES), lambda p, i: (0, p))
    return pl.pallas_call(
        _sb_kernel,
        grid=(pairs, seq // tq),
        in_specs=[q_spec, kv_spec, kv_spec],
        out_specs=pl.BlockSpec((tq, LANES), lambda p, i: (i, p)),
        out_shape=jax.ShapeDtypeStruct((seq, w), jnp.float32),
        scratch_shapes=[pltpu.VMEM((2, tq, LANES), jnp.float32),
                        pltpu.VMEM((2, tq, LANES), jnp.float32)],
        compiler_params=pltpu.CompilerParams(vmem_limit_bytes=VMEM_LIMIT),
        name="stick_break",
    )(q, k, v)


def _top_rows(problems, k):
    t = problems[0][0].shape[1]
    slot = lax.broadcasted_iota(jnp.int32, (k, t), 0)
    ss = [s for s, _ in problems]
    labels = [label for _, label in problems]
    vals = [jnp.zeros((k, t), jnp.float32) for _ in problems]
    labs = [jnp.zeros((k, t), jnp.float32) for _ in problems]
    for r in range(k):
        ms = [jnp.max(s, axis=0, keepdims=True) for s in ss]
        ls = [jnp.min(jnp.where(s == m, label, jnp.inf), axis=0, keepdims=True)
              for s, m, label in zip(ss, ms, labels)]
        vals = [jnp.where(slot == r, m, v) for m, v in zip(ms, vals)]
        labs = [jnp.where(slot == r, lab, v) for lab, v in zip(ls, labs)]
        ss = [jnp.where(label == lab, -jnp.inf, s) for s, lab, label in zip(ss, ls, labels)]
    return list(zip(vals, labs))


def _pair_candidates(v1, v2):
    t = v1.shape[1]
    sub = lax.broadcasted_iota(jnp.int32, (SUBLANES, t), 0)
    sub_f = sub.astype(jnp.float32)
    sums, poss = [], []
    for i in range(SUBLANES):
        reach = TOPK // (i + 1)
        for j0 in range(0, reach, SUBLANES):
            piece = v1[i:i + 1, :] + v2[j0:j0 + SUBLANES, :]
            if reach - j0 < SUBLANES:
                piece = jnp.where(sub < reach - j0, piece, -jnp.inf)
            sums.append(piece)
            poss.append(sub_f + float(i * TOPK + j0))
    sums.append(v1[SUBLANES:TOPK, :] + v2[0:1, :])
    poss.append((sub_f + float(SUBLANES)) * float(TOPK))
    return jnp.concatenate(sums, axis=0), jnp.concatenate(poss, axis=0)


def _mix_out_kernel(x_ref, ya_ref, yb_ref, mod_ref, nb_ref, wout_ref, nf_ref, wq_ref, k1_ref, k2_ref,
                    x1_ref, h2_ref, eid_ref, gate_ref, q_scr, eid_scr, gate_scr):
    gw = ya_ref.shape[1]
    mod = mod_ref[...]
    ya = ya_ref[...].astype(jnp.bfloat16)
    yb = _rms(yb_ref[...], nb_ref[...]).astype(jnp.bfloat16)
    y = (jnp.dot(ya, wout_ref[0:gw, :], preferred_element_type=jnp.float32)
         + jnp.dot(yb, wout_ref[gw:2 * gw, :], preferred_element_type=jnp.float32))
    x1 = x_ref[...] + mod[2:3, :] * y
    x1_ref[...] = x1
    h2 = (_rms(x1, nf_ref[...]) * (1.0 + mod[4:5, :]) + mod[3:4, :]).astype(jnp.bfloat16)
    q_scr[...] = jnp.dot(h2, wq_ref[...], preferred_element_type=jnp.float32)
    half = h2.shape[1] // 2
    bits = lax.bitcast_convert_type(h2.astype(jnp.float32), jnp.int32)
    h2_ref[...] = jnp.bitwise_or(jnp.bitwise_and(bits[:, half:], jnp.int32(-65536)),
                                 lax.shift_right_logical(bits[:, :half], 16))

    k1 = k1_ref[...].astype(jnp.bfloat16)
    k2 = k2_ref[...].astype(jnp.bfloat16)
    t = LANES
    key_f = lax.broadcasted_iota(jnp.int32, (N_KEYS, t), 0).astype(jnp.float32)
    slot = lax.broadcasted_iota(jnp.int32, (TOPK, t), 0)
    slot_f = slot.astype(jnp.float32)
    nt = (((1,), (1,)), ((), ()))

    def sub_key_tops(h, r0):
        c0 = pl.multiple_of(h * 2 * PEER_HALF, 2 * PEER_HALF)
        qa = q_scr[pl.ds(r0, t), pl.ds(c0, PEER_HALF)].astype(jnp.bfloat16)
        qb = q_scr[pl.ds(r0, t), pl.ds(c0 + PEER_HALF, PEER_HALF)].astype(jnp.bfloat16)
        s1 = lax.dot_general(k1, qa, nt, preferred_element_type=jnp.float32)
        s2 = lax.dot_general(k2, qb, nt, preferred_element_type=jnp.float32)
        return _top_rows([(s1, key_f), (s2, key_f)], TOPK)

    def route(i, _):
        heads = (2 * (i % (PEER_HEADS // 2)), 2 * (i % (PEER_HEADS // 2)) + 1)
        r0 = pl.multiple_of((i // (PEER_HEADS // 2)) * t, t)
        halves = [sub_key_tops(h, r0) for h in heads]
        tops = _top_rows([_pair_candidates(hv[0][0], hv[1][0]) for hv in halves], TOPK)
        for h, ((_, i1), (_, i2)), (top_s, pos) in zip(heads, halves, tops):
            eid = jnp.zeros((TOPK, t), jnp.float32)
            for r in range(TOPK):
                pr = pos[r:r + 1, :]
                pi = jnp.floor(pr * (1.0 / TOPK))
                pj = pr - pi * TOPK
                e1 = jnp.sum(jnp.where(slot_f == pi, i1, 0.0), axis=0, keepdims=True)
                e2 = jnp.sum(jnp.where(slot_f == pj, i2, 0.0), axis=0, keepdims=True)
                eid = jnp.where(slot == r, e1 * N_KEYS + e2, eid)
            ex = jnp.exp(top_s - top_s[0:1, :])
            gate = ex / jnp.sum(ex, axis=0, keepdims=True)
            p0 = pl.multiple_of(h * TOPK, TOPK)
            eid_scr[pl.ds(p0, TOPK), pl.ds(r0, t)] = eid
            gate_scr[pl.ds(p0, TOPK), pl.ds(r0, t)] = gate
        return 0

    lax.fori_loop(0, (PEER_HEADS // 2) * (x1.shape[0] // t), route, 0)
    eid_ref[...] = eid_scr[...].T.astype(jnp.int32)
    gate_ref[...] = gate_scr[...].T


def _mix_out(xb, ya, yb, mod_b, out_norm_b, w_out_bf, norm_ffn, wq_bf, k1, k2):
    seq, d = xb.shape
    gw = ya.shape[1]
    tt = min(OUT_TOKENS, seq)
    qcols = wq_bf.shape[1]
    tok_spec = lambda w: pl.BlockSpec((tt, w), lambda i: (i, 0))
    return pl.pallas_call(
        _mix_out_kernel,
        grid=(seq // tt,),
        in_specs=[tok_spec(d), tok_spec(gw), tok_spec(gw), _full(mod_b), _full(out_norm_b), _full(w_out_bf),
                  _full(norm_ffn), _full(wq_bf), _full(k1), _full(k2)],
        out_specs=[tok_spec(d), tok_spec(d // 2), tok_spec(PICKS), tok_spec(PICKS)],
        out_shape=[
            jax.ShapeDtypeStruct((seq, d), jnp.float32),
            jax.ShapeDtypeStruct((seq, d // 2), jnp.int32),
            jax.ShapeDtypeStruct((seq, PICKS), jnp.int32),
            jax.ShapeDtypeStruct((seq, PICKS), jnp.float32),
        ],
        scratch_shapes=[pltpu.VMEM((tt, qcols), jnp.float32), pltpu.VMEM((PICKS, tt), jnp.float32),
                        pltpu.VMEM((PICKS, tt), jnp.float32)],
        compiler_params=pltpu.CompilerParams(vmem_limit_bytes=VMEM_LIMIT),
        name="mix_out",
    )(xb, ya, yb, mod_b, out_norm_b, w_out_bf, norm_ffn, wq_bf, k1, k2)


_PAIRS = plsc.PackFormat.INTERLEAVED


def _sum4_bf16(p):
    return plsc.unpack((p[0] + p[1]) + (p[2] + p[3]), format=_PAIRS)


def _peer_experts(u_tab, v_tab, idx, gates, h2p):
    seq, width = h2p.shape
    d = 2 * width
    bf = jnp.bfloat16
    info = plsc.get_sparse_core_info()
    lanes = info.num_lanes
    workers = info.num_cores * info.num_subcores
    tokens = seq // workers
    steps = 2 * tokens
    assert tokens * workers == seq and tokens % 2 == 0 and PICKS == 2 * SC_ROWS and u_tab.shape[1] == width
    kchunks = width // lanes
    mesh = plsc.VectorSubcoreMesh(core_axis_name="c", subcore_axis_name="s")
    rows_t = pltpu.VMEM((SC_ROWS, width), jnp.int32)
    words_t = pltpu.VMEM((width,), jnp.int32)
    vec = lambda m: pltpu.VMEM((m,), jnp.float32)
    sem = pltpu.SemaphoreType.DMA
    c_gelu = float(np.sqrt(2.0 / np.pi))

    @functools.partial(
        pl.kernel, mesh=mesh,
        compiler_params=pltpu.CompilerParams(needs_layout_passes=False),
        out_type=jax.ShapeDtypeStruct((seq, d), jnp.float32),
        scratch_types=[pltpu.VMEM((steps, SC_ROWS), jnp.int32), rows_t, rows_t,
                       words_t, words_t, vec(PICKS), vec(PICKS), vec(d), vec(d), vec(PICKS), vec(PICKS)]
                      + [sem] * 8,
    )
    def experts(u_hbm, v_hbm, idx_hbm, gate_hbm, h2_hbm, out_hbm, idx_v, rows0, rows1,
                x0, x1, gt0, gt1, o0, o1, act_v, w_v, g0, g1, sx0, sx1, sg0, sg1, so0, so1):
        wid = lax.axis_index("s") * info.num_cores + lax.axis_index("c")
        tok0 = wid * tokens
        pltpu.sync_copy(idx_hbm.at[wid], idx_v)
        lane = lax.iota(jnp.int32, lanes)
        xs, gts, outs = (x0, x1), (gt0, gt1), (o0, o1)
        sxs, sgs, sos = (sx0, sx1), (sg0, sg1), (so0, so1)

        def fetch(tab, i, rows, s):
            return pltpu.make_async_copy(tab.at[idx_v.at[i]], rows, s)

        def xcopy(tk, par):
            return pltpu.make_async_copy(h2_hbm.at[tok0 + tk], xs[par], sxs[par])

        def gcopy(tk, par):
            return pltpu.make_async_copy(gate_hbm.at[tok0 + tk], gts[par], sgs[par])

        def ocopy(tk, par):
            return pltpu.make_async_copy(outs[par], out_hbm.at[tok0 + tk], sos[par])

        def dots(rows, xv, aoff):
            @pl.loop(0, SC_ROWS, step=lanes)
            def _(g):
                res = jnp.zeros((lanes,), jnp.float32)
                for rb in range(0, lanes, SC_DOT_ROWS):
                    def body(k4, accs):
                        off = pl.multiple_of(k4 * (4 * lanes), 4 * lanes)
                        xb = [plsc.bitcast(xv[pl.ds(off + c * lanes, lanes)], bf) for c in range(4)]
                        new = []
                        for r in range(SC_DOT_ROWS):
                            lo, hi = _sum4_bf16([
                                plsc.bitcast(rows[g + rb + r, pl.ds(off + c * lanes, lanes)], bf) * xb[c]
                                for c in range(4)])
                            new.append(accs[r] + (lo + hi))
                        return tuple(new)
                    zeros = tuple(jnp.zeros((lanes,), jnp.float32) for _ in range(SC_DOT_ROWS))
                    accs = lax.fori_loop(0, kchunks // 4, body, zeros)
                    for r in range(SC_DOT_ROWS):
                        res = jnp.where(lane == rb + r, jnp.sum(accs[r]), res)
                act_v[pl.ds(pl.multiple_of(aoff + g, lanes), lanes)] = res

        def weights(gv):
            for c in range(PICKS // lanes):
                a = act_v[pl.ds(c * lanes, lanes)]
                y = c_gelu * (a + 0.044715 * (a * a * a))
                th = 1.0 - 2.0 / (1.0 + jnp.exp(2.0 * y))
                w_v[pl.ds(c * lanes, lanes)] = gv[pl.ds(c * lanes, lanes)] * (0.5 * a * (1.0 + th))

        def combine(rows, woff, ov, first):
            for cc in range(0, kchunks, SC_COLS):
                def body(j4, accs):
                    j = j4 * 4
                    wb = []
                    for r in range(4):
                        wj = plsc.load_gather(w_v, [jnp.full((lanes,), woff, jnp.int32) + (j + r)])
                        wb.append(plsc.pack(wj, wj, format=_PAIRS))
                    new = []
                    for c in range(SC_COLS):
                        lo, hi = _sum4_bf16([
                            plsc.bitcast(rows[j + r, pl.ds((cc + c) * lanes, lanes)], bf) * wb[r]
                            for r in range(4)])
                        new.append(accs[2 * c] + lo)
                        new.append(accs[2 * c + 1] + hi)
                    return tuple(new)
                zeros = tuple(jnp.zeros((lanes,), jnp.float32) for _ in range(2 * SC_COLS))
                accs = lax.fori_loop(0, SC_ROWS // 4, body, zeros)
                for c in range(SC_COLS):
                    lo_sl = pl.ds((cc + c) * lanes, lanes)
                    hi_sl = pl.ds(width + (cc + c) * lanes, lanes)
                    if first:
                        ov[lo_sl] = accs[2 * c]
                        ov[hi_sl] = accs[2 * c + 1]
                    else:
                        ov[lo_sl] = ov[lo_sl] + accs[2 * c]
                        ov[hi_sl] = ov[hi_sl] + accs[2 * c + 1]

        def token(tk, par):
            xv, gv, ov = xs[par], gts[par], outs[par]
            fetch(u_hbm, 2 * tk + 1, rows1, g1).start()
            fetch(u_hbm, 2 * tk, rows0, g0).wait()
            xcopy(tk, par).wait()
            dots(rows0, xv, 0)
            fetch(v_hbm, 2 * tk, rows0, g0).start()
            fetch(u_hbm, 2 * tk + 1, rows1, g1).wait()
            dots(rows1, xv, SC_ROWS)
            gcopy(tk, par).wait()
            weights(gv)
            fetch(v_hbm, 2 * tk + 1, rows1, g1).start()
            fetch(v_hbm, 2 * tk, rows0, g0).wait()

            @pl.when(tk >= 2)
            def _():
                ocopy(tk - 2, par).wait()
            combine(rows0, 0, ov, True)

            @pl.when(tk + 1 < tokens)
            def _():
                fetch(u_hbm, 2 * tk + 2, rows0, g0).start()
                xcopy(tk + 1, 1 - par).start()
                gcopy(tk + 1, 1 - par).start()
            fetch(v_hbm, 2 * tk + 1, rows1, g1).wait()
            combine(rows1, SC_ROWS, ov, False)
            ocopy(tk, par).start()

        fetch(u_hbm, 0, rows0, g0).start()
        xcopy(0, 0).start()
        gcopy(0, 0).start()

        @pl.loop(0, tokens, step=2)
        def _(tk):
            token(tk, 0)
            token(tk + 1, 1)

        ocopy(tokens - 2, 0).wait()
        ocopy(tokens - 1, 1).wait()

    return experts(u_tab, v_tab, idx.reshape(workers, steps, SC_ROWS), gates, h2p)


def _final_kernel(x_ref, peer_ref, g2_ref, g_ref, o_ref):
    o_ref[...] = _rms(x_ref[...] + g2_ref[...] * peer_ref[...], g_ref[...])


def _final_norm(x1, peer, gate2, gain):
    seq, d = x1.shape
    tt = min(1024, seq)
    tok_spec = pl.BlockSpec((tt, d), lambda i: (i, 0))
    return pl.pallas_call(
        _final_kernel,
        grid=(seq // tt,),
        in_specs=[tok_spec, tok_spec, _full(gate2), _full(gain)],
        out_specs=tok_spec,
        out_shape=jax.ShapeDtypeStruct((seq, d), jnp.float32),
        name="final_norm",
    )(x1, peer, gate2, gain)


def _pack_table(tab):
    half = tab.shape[1] // 2
    bits = lax.bitcast_convert_type(tab.astype(jnp.bfloat16), jnp.uint16).astype(jnp.uint32)
    words = bits[:, :half] | (bits[:, half:] << 16)
    return lax.bitcast_convert_type(words, jnp.int32)


def kernel(x, c, ada_w, ada_b, norm_mix, norm_ffn, w_in, gm_ws, gm_bs, gm_vnorm, out_norm_a, out_norm_b,
           w_out, peer_wq, peer_k1, peer_k2, peer_u, peer_v, final_norm):
    batch, seq, d = x.shape
    depth = ada_w.shape[0]
    bf = jnp.bfloat16
    mod = _ada_mod(c, ada_w, ada_b).reshape(depth, batch, 6, d)
    state = [(x[b], None, None) for b in range(batch)]
    for l in range(depth):
        row = lambda a: a[l].reshape(1, -1)
        w_in_bf, w_out_bf, wq_bf = w_in[l].astype(bf), w_out[l].astype(bf), peer_wq[l].astype(bf)
        gm_bst = gm_bs[l].T
        u_pack = _pack_table(peer_u[l])
        v_pack = _pack_table(peer_v[l])
        for b in range(batch):
            mod_b = mod[l, b]
            ya, q, k, v, xb = _mix_in(*state[b], mod_b, row(norm_mix), w_in_bf, gm_ws[l], gm_bst,
                                      row(gm_vnorm), row(out_norm_a))
            yb = _stick_break(q, k, v)
            x1, h2p, eid, gate = _mix_out(xb, ya, yb, mod_b, row(out_norm_b), w_out_bf, row(norm_ffn),
                                         wq_bf, peer_k1[l], peer_k2[l])
            state[b] = (x1, _peer_experts(u_pack, v_pack, eid, gate, h2p), mod_b[5:6])
    gain = final_norm.reshape(1, d)
    return jnp.stack([_final_norm(*st, gain) for st in state], axis=0)
```

```python
import functools

import jax
import jax.numpy as jnp
import numpy as np
from jax import lax
from jax.experimental import pallas as pl
from jax.experimental.pallas import tpu as pltpu
from jax.experimental.pallas import tpu_sc as plsc

EPS = 1e-6
CHUNK = 64
GM_HEADS = 8
GM_BLOCK = 128
HEAD_DIM = 64
PEER_HEADS = 8
PEER_HALF = 128
N_KEYS = 128
TOPK = 16
PICKS = PEER_HEADS * TOPK

LANES = 128
SUBLANES = 8
VMEM_LIMIT = 56 * 1024 * 1024

MIX_TOKENS = 512
SB_Q = 256
SB_K = 256
OUT_TOKENS = 256
SC_ROWS = 64
SC_RING = 2
SC_COLS = 8
SC_DOT_ROWS = 8
DENSE_SEQS = 2
DENSE_TOKENS = 1024
DENSE_EXPERTS = 512


def _gelu(x):
    return 0.5 * x * (1.0 + jnp.tanh(np.sqrt(2.0 / np.pi) * (x + 0.044715 * (x * x * x))))


def _rms(x, gain):
    return x * lax.rsqrt(jnp.mean(x * x, axis=-1, keepdims=True) + EPS) * gain


def _full(a):
    return pl.BlockSpec(a.shape, lambda *_: (0,) * a.ndim)


def _mod_kernel(c_ref, w_ref, b_ref, o_ref):
    c = c_ref[...]
    c_act = c * jax.nn.sigmoid(c)
    o_ref[...] = jnp.dot(c_act, w_ref[...], preferred_element_type=jnp.float32) + b_ref[...]


def _ada_mod(c, ada_w, ada_b):
    depth, d, cols = ada_w.shape
    b = c.shape[0]
    tn = cols // 4
    return pl.pallas_call(
        _mod_kernel,
        grid=(depth, cols // tn),
        in_specs=[
            pl.BlockSpec((b, d), lambda l, j: (0, 0)),
            pl.BlockSpec((None, d, tn), lambda l, j: (l, 0, j)),
            pl.BlockSpec((None, 1, tn), lambda l, j: (l, 0, j)),
        ],
        out_specs=pl.BlockSpec((None, b, tn), lambda l, j: (l, 0, j)),
        out_shape=jax.ShapeDtypeStruct((depth, b, cols), jnp.float32),
        compiler_params=pltpu.CompilerParams(vmem_limit_bytes=VMEM_LIMIT),
        name="ada_mod",
    )(c, ada_w, ada_b.reshape(depth, 1, cols))


def _mix_in_kernel(has_peer, *refs):
    if has_peer:
        (x_ref, peer_ref, g2_ref, mod_ref, nm_ref, win_ref, ws_ref, bst_ref, vn_ref, na_ref,
         ya_ref, q_ref, k_ref, v_ref, xo_ref) = refs
        x = x_ref[...] + g2_ref[...] * peer_ref[...]
        xo_ref[...] = x
    else:
        (x_ref, mod_ref, nm_ref, win_ref, ws_ref, bst_ref, vn_ref, na_ref,
         ya_ref, q_ref, k_ref, v_ref) = refs
        x = x_ref[...]
    gw = GM_HEADS * HEAD_DIM
    mod = mod_ref[...]
    h = _rms(x, nm_ref[...]) * (1.0 + mod[1:2, :]) + mod[0:1, :]
    proj = jnp.dot(h.astype(jnp.bfloat16), win_ref[...], preferred_element_type=jnp.float32)
    u = _gelu(proj[:, 0:gw])
    v = _gelu(proj[:, gw:2 * gw])
    q_ref[...] = proj[:, 2 * gw:3 * gw].astype(jnp.bfloat16)
    k_ref[...] = proj[:, 3 * gw:4 * gw].astype(jnp.bfloat16)
    v_ref[...] = proj[:, 4 * gw:5 * gw].astype(jnp.bfloat16)

    r = lax.broadcasted_iota(jnp.int32, (gw, gw), 0) // HEAD_DIM
    cc = lax.broadcasted_iota(jnp.int32, (gw, gw), 1) // HEAD_DIM
    avg = jnp.where(r == cc, 1.0 / HEAD_DIM, 0.0).astype(jnp.bfloat16)
    ms = jnp.dot((v * v).astype(jnp.bfloat16), avg, preferred_element_type=jnp.float32)
    vh = (v * lax.rsqrt(ms + EPS) * vn_ref[...]).astype(jnp.bfloat16)

    t_chunk = lax.broadcasted_iota(jnp.int32, (GM_BLOCK, GM_BLOCK), 0) // CHUNK
    s_chunk = lax.broadcasted_iota(jnp.int32, (GM_BLOCK, GM_BLOCK), 1) // CHUNK
    keep = s_chunk <= t_chunk
    lane = lax.broadcasted_iota(jnp.int32, (GM_BLOCK, LANES), 1)
    first = lane < HEAD_DIM
    bst = bst_ref[...]
    n_blocks = x.shape[0] // GM_BLOCK
    for p in range(GM_HEADS // 2):
        w0 = jnp.where(keep, ws_ref[2 * p], 0.0).astype(jnp.bfloat16)
        w1 = jnp.where(keep, ws_ref[2 * p + 1], 0.0).astype(jnp.bfloat16)
        bias = jnp.where(first, bst[:, 2 * p:2 * p + 1], bst[:, 2 * p + 1:2 * p + 2])
        for nb in range(n_blocks):
            rows = slice(nb * GM_BLOCK, (nb + 1) * GM_BLOCK)
            cols = slice(p * LANES, (p + 1) * LANES)
            vp = vh[rows, cols]
            z0 = jnp.dot(w0, vp, preferred_element_type=jnp.float32)
            z1 = jnp.dot(w1, vp, preferred_element_type=jnp.float32)
            z = jnp.where(first, z0, z1) + bias
            ya_ref[rows, cols] = u[rows, cols] * z
    ya = ya_ref[...]
    ya_ref[...] = _rms(ya, na_ref[...])


def _mix_in(xb, peer, gate2, mod_b, norm_mix, w_in_bf, gm_ws, gm_bst, gm_vnorm, out_norm_a):
    seq, d = xb.shape
    gw = GM_HEADS * HEAD_DIM
    ts = min(MIX_TOKENS, seq)
    tok_spec = lambda w: pl.BlockSpec((ts, w), lambda i: (i, 0))
    has_peer = peer is not None
    weights = (mod_b, norm_mix, w_in_bf, gm_ws, gm_bst, gm_vnorm, out_norm_a)
    acts = (xb, peer, gate2) if has_peer else (xb,)
    act_specs = [tok_spec(d), tok_spec(d), _full(gate2)] if has_peer else [tok_spec(d)]
    out_specs = [tok_spec(gw)] * 4 + ([tok_spec(d)] if has_peer else [])
    out_shape = ([jax.ShapeDtypeStruct((seq, gw), jnp.float32)]
                 + [jax.ShapeDtypeStruct((seq, gw), jnp.bfloat16)] * 3
                 + ([jax.ShapeDtypeStruct((seq, d), jnp.float32)] if has_peer else []))
    outs = pl.pallas_call(
        functools.partial(_mix_in_kernel, has_peer),
        grid=(seq // ts,),
        in_specs=act_specs + [_full(w) for w in weights],
        out_specs=out_specs,
        out_shape=out_shape,
        compiler_params=pltpu.CompilerParams(vmem_limit_bytes=VMEM_LIMIT),
        name="mix_in",
    )(*acts, *weights)
    return tuple(outs) if has_peer else (*outs, xb)


def _sb_logs(qh, k, mask):
    z = lax.dot_general(qh, k, (((1,), (1,)), ((), ())), preferred_element_type=jnp.float32)
    soft = jnp.log(1.0 + jnp.exp(-jnp.abs(z)))
    log_beta = jnp.minimum(z, 0.0) - soft
    log_keep = log_beta - z
    if mask is not None:
        log_keep = jnp.where(mask, log_keep, 0.0)
    hi = log_keep.astype(jnp.bfloat16)
    lo = (log_keep - hi.astype(jnp.float32)).astype(jnp.bfloat16)
    return log_beta, log_keep, hi, lo


def _sb_later(hi, lo, tri):
    return (jnp.dot(hi, tri, preferred_element_type=jnp.float32)
            + jnp.dot(lo, tri, preferred_element_type=jnp.float32))


def _sb_step(qs, kvs, tri, carries, mask):
    logs = [[_sb_logs(qh, k, mask) for qh in qs] for k, _ in kvs]
    laters = [[_sb_later(lg[2], lg[3], tri) for lg in blk] for blk in logs]
    res = []
    for hh, carry in enumerate(carries):
        out = None
        for (_, v), blk, lat in zip(kvs, logs, laters):
            lg, later = blk[hh], lat[hh]
            a = jnp.exp(lg[0] + later + carry)
            if mask is not None:
                a = jnp.where(mask, a, 0.0)
            part = jnp.dot(a.astype(jnp.bfloat16), v, preferred_element_type=jnp.float32)
            out = part if out is None else out + part
            carry = carry + later[:, 0:1] + lg[1][:, 0:1]
        res.append((out, carry))
    return res


def _sb_kernel(q_ref, k_ref, v_ref, o_ref, acc_ref, car_ref):
    tq, tk = q_ref.shape[0], SB_K
    qi = pl.program_id(1)
    lane = lax.broadcasted_iota(jnp.int32, (tq, LANES), 1)
    first = lane < HEAD_DIM
    q = q_ref[...] * (1.0 / np.sqrt(HEAD_DIM))
    zero = jnp.zeros_like(q)
    qs = (jnp.where(first, q, zero), jnp.where(first, zero, q))
    tri = (lax.broadcasted_iota(jnp.int32, (tk, tk), 0)
           > lax.broadcasted_iota(jnp.int32, (tk, tk), 1)).astype(jnp.bfloat16)
    causal = (lax.broadcasted_iota(jnp.int32, (tq, tk), 1)
              < lax.broadcasted_iota(jnp.int32, (tq, tk), 0))

    def kv(block):
        off = pl.multiple_of(block * tk, tk)
        return k_ref[pl.ds(off, tk), :], v_ref[pl.ds(off, tk), :]

    def accumulate(res, assign=False):
        for hh in range(2):
            acc_ref[hh] = res[hh][0] if assign else acc_ref[hh] + res[hh][0]
            car_ref[hh] = jnp.broadcast_to(res[hh][1], (tq, LANES))

    def carries():
        return car_ref[0][:, 0:1], car_ref[1][:, 0:1]

    zero_carry = jnp.zeros((tq, 1), jnp.float32)
    accumulate(_sb_step(qs, [kv(qi)], tri, (zero_carry, zero_carry), causal), assign=True)

    def body(i, _):
        accumulate(_sb_step(qs, [kv(qi - 1 - 2 * i), kv(qi - 2 - 2 * i)], tri, carries(), None))
        return 0

    lax.fori_loop(0, qi // 2, body, 0)

    @pl.when(qi % 2 == 1)
    def _():
        accumulate(_sb_step(qs, [kv(0)], tri, carries(), None))

    o_ref[...] = jnp.where(first, acc_ref[0], acc_ref[1])


def _stick_break(q, k, v):
    seq, w = q.shape
    pairs = w // LANES
    tq = min(SB_Q, seq)
    assert tq == min(SB_K, seq)
    q_spec = pl.BlockSpec((tq, LANES), lambda p, i: (i, p))
    kv_spec = pl.BlockSpec((seq, LANES), lambda p, i: (0, p))
    return pl.pallas_call(
        _sb_kernel,
        grid=(pairs, seq // tq),
        in_specs=[q_spec, kv_spec, kv_spec],
        out_specs=pl.BlockSpec((tq, LANES), lambda p, i: (i, p)),
        out_shape=jax.ShapeDtypeStruct((seq, w), jnp.float32),
        scratch_shapes=[pltpu.VMEM((2, tq, LANES), jnp.float32),
                        pltpu.VMEM((2, tq, LANES), jnp.float32)],
        compiler_params=pltpu.CompilerParams(vmem_limit_bytes=VMEM_LIMIT),
        name="stick_break",
    )(q, k, v)


def _top_rows(problems, k):
    t = problems[0][0].shape[1]
    slot = lax.broadcasted_iota(jnp.int32, (k, t), 0)
    ss = [s for s, _ in problems]
    labels = [label for _, label in problems]
    vals = [jnp.zeros((k, t), jnp.float32) for _ in problems]
    labs = [jnp.zeros((k, t), jnp.float32) for _ in problems]
    for r in range(k):
        ms = [jnp.max(s, axis=0, keepdims=True) for s in ss]
        ls = [jnp.min(jnp.where(s == m, label, jnp.inf), axis=0, keepdims=True)
              for s, m, label in zip(ss, ms, labels)]
        vals = [jnp.where(slot == r, m, v) for m, v in zip(ms, vals)]
        labs = [jnp.where(slot == r, lab, v) for lab, v in zip(ls, labs)]
        ss = [jnp.where(label == lab, -jnp.inf, s) for s, lab, label in zip(ss, ls, labels)]
    return list(zip(vals, labs))


def _pair_candidates(v1, v2):
    t = v1.shape[1]
    sub = lax.broadcasted_iota(jnp.int32, (SUBLANES, t), 0)
    sub_f = sub.astype(jnp.float32)
    sums, poss = [], []
    for i in range(SUBLANES):
        reach = TOPK // (i + 1)
        for j0 in range(0, reach, SUBLANES):
            piece = v1[i:i + 1, :] + v2[j0:j0 + SUBLANES, :]
            if reach - j0 < SUBLANES:
                piece = jnp.where(sub < reach - j0, piece, -jnp.inf)
            sums.append(piece)
            poss.append(sub_f + float(i * TOPK + j0))
    sums.append(v1[SUBLANES:TOPK, :] + v2[0:1, :])
    poss.append((sub_f + float(SUBLANES)) * float(TOPK))
    return jnp.concatenate(sums, axis=0), jnp.concatenate(poss, axis=0)


def _mix_out_kernel(x_ref, ya_ref, yb_ref, mod_ref, nb_ref, wout_ref, nf_ref, wq_ref, k1_ref, k2_ref,
                    x1_ref, h2_ref, eid_ref, gate_ref, q_scr, eid_scr, gate_scr):
    gw = ya_ref.shape[1]
    mod = mod_ref[...]
    ya = ya_ref[...].astype(jnp.bfloat16)
    yb = _rms(yb_ref[...], nb_ref[...]).astype(jnp.bfloat16)
    y = (jnp.dot(ya, wout_ref[0:gw, :], preferred_element_type=jnp.float32)
         + jnp.dot(yb, wout_ref[gw:2 * gw, :], preferred_element_type=jnp.float32))
    x1 = x_ref[...] + mod[2:3, :] * y
    x1_ref[...] = x1
    h2 = (_rms(x1, nf_ref[...]) * (1.0 + mod[4:5, :]) + mod[3:4, :]).astype(jnp.bfloat16)
    q_scr[...] = jnp.dot(h2, wq_ref[...], preferred_element_type=jnp.float32)
    half = h2.shape[1] // 2
    bits = lax.bitcast_convert_type(h2.astype(jnp.float32), jnp.int32)
    h2_ref[...] = jnp.bitwise_or(jnp.bitwise_and(bits[:, half:], jnp.int32(-65536)),
                                 lax.shift_right_logical(bits[:, :half], 16))

    k1 = k1_ref[...].astype(jnp.bfloat16)
    k2 = k2_ref[...].astype(jnp.bfloat16)
    t = LANES
    key_f = lax.broadcasted_iota(jnp.int32, (N_KEYS, t), 0).astype(jnp.float32)
    slot = lax.broadcasted_iota(jnp.int32, (TOPK, t), 0)
    slot_f = slot.astype(jnp.float32)
    nt = (((1,), (1,)), ((), ()))

    def sub_key_tops(h, r0):
        c0 = pl.multiple_of(h * 2 * PEER_HALF, 2 * PEER_HALF)
        qa = q_scr[pl.ds(r0, t), pl.ds(c0, PEER_HALF)].astype(jnp.bfloat16)
        qb = q_scr[pl.ds(r0, t), pl.ds(c0 + PEER_HALF, PEER_HALF)].astype(jnp.bfloat16)
        s1 = lax.dot_general(k1, qa, nt, preferred_element_type=jnp.float32)
        s2 = lax.dot_general(k2, qb, nt, preferred_element_type=jnp.float32)
        return _top_rows([(s1, key_f), (s2, key_f)], TOPK)

    def route(i, _):
        heads = (2 * (i % (PEER_HEADS // 2)), 2 * (i % (PEER_HEADS // 2)) + 1)
        r0 = pl.multiple_of((i // (PEER_HEADS // 2)) * t, t)
        halves = [sub_key_tops(h, r0) for h in heads]
        tops = _top_rows([_pair_candidates(hv[0][0], hv[1][0]) for hv in halves], TOPK)
        for h, ((_, i1), (_, i2)), (top_s, pos) in zip(heads, halves, tops):
            eid = jnp.zeros((TOPK, t), jnp.float32)
            for r in range(TOPK):
                pr = pos[r:r + 1, :]
                pi = jnp.floor(pr * (1.0 / TOPK))
                pj = pr - pi * TOPK
                e1 = jnp.sum(jnp.where(slot_f == pi, i1, 0.0), axis=0, keepdims=True)
                e2 = jnp.sum(jnp.where(slot_f == pj, i2, 0.0), axis=0, keepdims=True)
                eid = jnp.where(slot == r, e1 * N_KEYS + e2, eid)
            ex = jnp.exp(top_s - top_s[0:1, :])
            gate = ex / jnp.sum(ex, axis=0, keepdims=True)
            p0 = pl.multiple_of(h * TOPK, TOPK)
            eid_scr[pl.ds(p0, TOPK), pl.ds(r0, t)] = eid
            gate_scr[pl.ds(p0, TOPK), pl.ds(r0, t)] = gate
        return 0

    lax.fori_loop(0, (PEER_HEADS // 2) * (x1.shape[0] // t), route, 0)
    eid_ref[...] = eid_scr[...].T.astype(jnp.int32)
    gate_ref[...] = gate_scr[...].T


def _mix_out(xb, ya, yb, mod_b, out_norm_b, w_out_bf, norm_ffn, wq_bf, k1, k2):
    seq, d = xb.shape
    gw = ya.shape[1]
    tt = min(OUT_TOKENS, seq)
    qcols = wq_bf.shape[1]
    tok_spec = lambda w: pl.BlockSpec((tt, w), lambda i: (i, 0))
    return pl.pallas_call(
        _mix_out_kernel,
        grid=(seq // tt,),
        in_specs=[tok_spec(d), tok_spec(gw), tok_spec(gw), _full(mod_b), _full(out_norm_b), _full(w_out_bf),
                  _full(norm_ffn), _full(wq_bf), _full(k1), _full(k2)],
        out_specs=[tok_spec(d), tok_spec(d // 2), tok_spec(PICKS), tok_spec(PICKS)],
        out_shape=[
            jax.ShapeDtypeStruct((seq, d), jnp.float32),
            jax.ShapeDtypeStruct((seq, d // 2), jnp.int32),
            jax.ShapeDtypeStruct((seq, PICKS), jnp.int32),
            jax.ShapeDtypeStruct((seq, PICKS), jnp.float32),
        ],
        scratch_shapes=[pltpu.VMEM((tt, qcols), jnp.float32), pltpu.VMEM((PICKS, tt), jnp.float32),
                        pltpu.VMEM((PICKS, tt), jnp.float32)],
        compiler_params=pltpu.CompilerParams(vmem_limit_bytes=VMEM_LIMIT),
        name="mix_out",
    )(xb, ya, yb, mod_b, out_norm_b, w_out_bf, norm_ffn, wq_bf, k1, k2)


_PAIRS = plsc.PackFormat.INTERLEAVED


def _sum4_bf16(p):
    return plsc.unpack((p[0] + p[1]) + (p[2] + p[3]), format=_PAIRS)


def _peer_experts(u_tab, v_tab, idx, gates, h2p):
    seq, width = h2p.shape
    d = 2 * width
    bf = jnp.bfloat16
    info = plsc.get_sparse_core_info()
    lanes = info.num_lanes
    workers = info.num_cores * info.num_subcores
    tokens = seq // workers
    assert tokens * workers == seq and tokens % 2 == 0 and PICKS == SC_RING * SC_ROWS
    assert u_tab.shape[1:] == (width // LANES, LANES)
    kchunks = width // lanes
    mesh = plsc.VectorSubcoreMesh(core_axis_name="c", subcore_axis_name="s")
    rows_t = pltpu.VMEM((SC_ROWS, width // LANES, LANES), jnp.int32)
    planes_per_k4 = LANES // (4 * lanes)
    words_t = pltpu.VMEM((width,), jnp.int32)
    vec = lambda m: pltpu.VMEM((m,), jnp.float32)
    sem = pltpu.SemaphoreType.DMA
    c_gelu = float(np.sqrt(2.0 / np.pi))

    @functools.partial(
        pl.kernel, mesh=mesh,
        compiler_params=pltpu.CompilerParams(needs_layout_passes=False),
        out_type=jax.ShapeDtypeStruct((seq, d), jnp.float32),
        scratch_types=[pltpu.VMEM((tokens, PICKS), jnp.int32)] + [rows_t] * SC_RING
                      + [words_t, words_t, vec(PICKS), vec(PICKS), vec(d), vec(d), vec(PICKS), vec(PICKS)]
                      + [sem] * (SC_RING + 6),
    )
    def experts(u_hbm, v_hbm, idx_hbm, gate_hbm, h2_hbm, out_hbm, idx_v, *scratch):
        ring, scratch = scratch[:SC_RING], scratch[SC_RING:]
        x0, x1, gt0, gt1, o0, o1, act_v, w_v = scratch[:8]
        gsems, (sx0, sx1, sg0, sg1, so0, so1) = scratch[8:8 + SC_RING], scratch[8 + SC_RING:]
        wid = lax.axis_index("s") * info.num_cores + lax.axis_index("c")
        tok0 = wid * tokens
        pltpu.sync_copy(idx_hbm.at[wid], idx_v)
        lane = lax.iota(jnp.int32, lanes)
        xs, gts, outs = (x0, x1), (gt0, gt1), (o0, o1)
        sxs, sgs, sos = (sx0, sx1), (sg0, sg1), (so0, so1)

        def fetch(tab, i, rows, s):
            picks = idx_v.at[i // SC_RING, pl.ds((i % SC_RING) * SC_ROWS, SC_ROWS)]
            return pltpu.make_async_copy(tab.at[picks], rows, s)

        def xcopy(tk, par):
            return pltpu.make_async_copy(h2_hbm.at[tok0 + tk], xs[par], sxs[par])

        def gcopy(tk, par):
            return pltpu.make_async_copy(gate_hbm.at[tok0 + tk], gts[par], sgs[par])

        def ocopy(tk, par):
            return pltpu.make_async_copy(outs[par], out_hbm.at[tok0 + tk], sos[par])

        def dots(rows, xv, aoff):
            @pl.loop(0, SC_ROWS, step=lanes)
            def _(g):
                res = jnp.zeros((lanes,), jnp.float32)
                for rb in range(0, lanes, SC_DOT_ROWS):
                    def body(k4, accs):
                        off = pl.multiple_of(k4 * (4 * lanes), 4 * lanes)
                        xb = [plsc.bitcast(xv[pl.ds(off + c * lanes, lanes)], bf) for c in range(4)]
                        new = []
                        for r in range(SC_DOT_ROWS):
                            lo, hi = _sum4_bf16([
                                plsc.bitcast(rows[g + rb + r, k4 // planes_per_k4,
                                                  pl.ds((k4 % planes_per_k4) * (4 * lanes) + c * lanes, lanes)],
                                             bf) * xb[c]
                                for c in range(4)])
                            new.append(accs[r] + (lo + hi))
                        return tuple(new)
                    zeros = tuple(jnp.zeros((lanes,), jnp.float32) for _ in range(SC_DOT_ROWS))
                    accs = lax.fori_loop(0, kchunks // 4, body, zeros)
                    for r in range(SC_DOT_ROWS):
                        res = jnp.where(lane == rb + r, jnp.sum(accs[r]), res)
                act_v[pl.ds(pl.multiple_of(aoff + g, lanes), lanes)] = res

        def weights(gv):
            for c in range(PICKS // lanes):
                a = act_v[pl.ds(c * lanes, lanes)]
                y = c_gelu * (a + 0.044715 * (a * a * a))
                th = 1.0 - 2.0 / (1.0 + jnp.exp(2.0 * y))
                w_v[pl.ds(c * lanes, lanes)] = gv[pl.ds(c * lanes, lanes)] * (0.5 * a * (1.0 + th))

        def combine(rows, woff, ov, first):
            for cc in range(0, kchunks, SC_COLS):
                def body(j4, accs):
                    j = j4 * 4
                    wb = []
                    for r in range(4):
                        wj = plsc.load_gather(w_v, [jnp.full((lanes,), woff, jnp.int32) + (j + r)])
                        wb.append(plsc.pack(wj, wj, format=_PAIRS))
                    new = []
                    for c in range(SC_COLS):
                        lo, hi = _sum4_bf16([
                            plsc.bitcast(rows[j + r, (cc + c) * lanes // LANES,
                                              pl.ds((cc + c) * lanes % LANES, lanes)], bf) * wb[r]
                            for r in range(4)])
                        new.append(accs[2 * c] + lo)
                        new.append(accs[2 * c + 1] + hi)
                    return tuple(new)
                zeros = tuple(jnp.zeros((lanes,), jnp.float32) for _ in range(2 * SC_COLS))
                accs = lax.fori_loop(0, SC_ROWS // 4, body, zeros)
                for c in range(SC_COLS):
                    lo_sl = pl.ds((cc + c) * lanes, lanes)
                    hi_sl = pl.ds(width + (cc + c) * lanes, lanes)
                    if first:
                        ov[lo_sl] = accs[2 * c]
                        ov[hi_sl] = accs[2 * c + 1]
                    else:
                        ov[lo_sl] = ov[lo_sl] + accs[2 * c]
                        ov[hi_sl] = ov[hi_sl] + accs[2 * c + 1]

        def token(tk, par):
            xv, gv, ov = xs[par], gts[par], outs[par]
            for s in range(2 * SC_RING):
                q = s % SC_RING
                rows, gsem = ring[q], gsems[q]
                if s < SC_RING:
                    fetch(u_hbm, SC_RING * tk + q, rows, gsem).wait()
                    if s == 0:
                        xcopy(tk, par).wait()
                    dots(rows, xv, q * SC_ROWS)
                    fetch(v_hbm, SC_RING * tk + q, rows, gsem).start()
                    if s == SC_RING - 1:
                        gcopy(tk, par).wait()
                        weights(gv)
                else:
                    fetch(v_hbm, SC_RING * tk + q, rows, gsem).wait()
                    if s == SC_RING:
                        @pl.when(tk >= 2)
                        def _():
                            ocopy(tk - 2, par).wait()
                    combine(rows, q * SC_ROWS, ov, s == SC_RING)

                    @pl.when(tk + 1 < tokens)
                    def _():
                        fetch(u_hbm, SC_RING * (tk + 1) + q, rows, gsem).start()
                        if s == SC_RING:
                            xcopy(tk + 1, 1 - par).start()
                            gcopy(tk + 1, 1 - par).start()
            ocopy(tk, par).start()

        for q in range(SC_RING):
            fetch(u_hbm, q, ring[q], gsems[q]).start()
        xcopy(0, 0).start()
        gcopy(0, 0).start()

        @pl.loop(0, tokens, step=2)
        def _(tk):
            token(tk, 0)
            token(tk + 1, 1)

        ocopy(tokens - 2, 0).wait()
        ocopy(tokens - 1, 1).wait()

    return experts(u_tab, v_tab, idx.reshape(workers, tokens, PICKS), gates, h2p)


def _gate_matrix(idx, gates, n_experts):
    seq = idx.shape[0]
    info = plsc.get_sparse_core_info()
    lanes = info.num_lanes
    workers = info.num_cores * info.num_subcores
    tokens = seq // workers
    assert tokens * workers == seq and tokens % 2 == 0 and n_experts % lanes == 0
    mesh = plsc.VectorSubcoreMesh(core_axis_name="c", subcore_axis_name="s")
    row_t = pltpu.VMEM((n_experts,), jnp.float32)
    sem = pltpu.SemaphoreType.DMA

    @functools.partial(
        pl.kernel, mesh=mesh,
        compiler_params=pltpu.CompilerParams(needs_layout_passes=False),
        out_type=jax.ShapeDtypeStruct((seq, n_experts), jnp.float32),
        scratch_types=[pltpu.VMEM((tokens, PICKS), jnp.int32), pltpu.VMEM((tokens, PICKS), jnp.float32),
                       row_t, row_t, sem, sem],
    )
    def build(idx_hbm, gate_hbm, out_hbm, idx_v, gate_v, row0, row1, s0, s1):
        wid = lax.axis_index("s") * info.num_cores + lax.axis_index("c")
        tok0 = wid * tokens
        pltpu.sync_copy(idx_hbm.at[wid], idx_v)
        pltpu.sync_copy(gate_hbm.at[wid], gate_v)
        rows, sems = (row0, row1), (s0, s1)
        zeros = jnp.zeros((lanes,), jnp.float32)

        @pl.loop(0, n_experts, step=lanes)
        def _(i):
            row0[pl.ds(i, lanes)] = zeros
            row1[pl.ds(i, lanes)] = zeros

        def ocopy(tk, par):
            return pltpu.make_async_copy(rows[par], out_hbm.at[tok0 + tk], sems[par])

        def token(tk, par):
            row = rows[par]

            @pl.when(tk >= 2)
            def _():
                ocopy(tk - 2, par).wait()
                for c in range(PICKS // lanes):
                    plsc.store_scatter(row, [idx_v[tk - 2, pl.ds(c * lanes, lanes)]], zeros)
            for c in range(PICKS // lanes):
                sl = pl.ds(c * lanes, lanes)
                plsc.addupdate_scatter(row, [idx_v[tk, sl]], gate_v[tk, sl])
            ocopy(tk, par).start()

        @pl.loop(0, tokens, step=2)
        def _(tk):
            token(tk, 0)
            token(tk + 1, 1)

        ocopy(tokens - 2, 0).wait()
        ocopy(tokens - 1, 1).wait()

    shape = (workers, tokens, PICKS)
    return build(idx.reshape(shape), gates.reshape(shape))


def _dense_kernel(h2p_ref, g_ref, u_ref, v_ref, o_ref, h2_scr):
    half = h2p_ref.shape[1]

    @pl.when(pl.program_id(1) == 0)
    def _():
        words = h2p_ref[...]
        lo = lax.bitcast_convert_type(lax.shift_left(words, 16), jnp.float32)
        hi = lax.bitcast_convert_type(jnp.bitwise_and(words, jnp.int32(-65536)), jnp.float32)
        h2_scr[:, 0:half] = lo.astype(jnp.bfloat16)
        h2_scr[:, half:2 * half] = hi.astype(jnp.bfloat16)
        o_ref[...] = jnp.zeros_like(o_ref)

    s = lax.dot_general(h2_scr[...], u_ref[...], (((1,), (1,)), ((), ())),
                        preferred_element_type=jnp.float32)
    w = (g_ref[...] * _gelu(s)).astype(jnp.bfloat16)
    o_ref[...] += jnp.dot(w, v_ref[...], preferred_element_type=jnp.float32)


def _peer_dense(h2p, gmat, u_bf, v_bf):
    seq, half = h2p.shape
    n_experts, d = u_bf.shape
    tt = min(DENSE_TOKENS, seq)
    te = min(DENSE_EXPERTS, n_experts)
    return pl.pallas_call(
        _dense_kernel,
        grid=(seq // tt, n_experts // te),
        in_specs=[pl.BlockSpec((tt, half), lambda i, e: (i, 0)),
                  pl.BlockSpec((tt, te), lambda i, e: (i, e)),
                  pl.BlockSpec((te, d), lambda i, e: (e, 0)),
                  pl.BlockSpec((te, d), lambda i, e: (e, 0))],
        out_specs=pl.BlockSpec((tt, d), lambda i, e: (i, 0)),
        out_shape=jax.ShapeDtypeStruct((seq, d), jnp.float32),
        scratch_shapes=[pltpu.VMEM((tt, d), jnp.bfloat16)],
        compiler_params=pltpu.CompilerParams(vmem_limit_bytes=VMEM_LIMIT),
        name="peer_dense",
    )(h2p, gmat, u_bf, v_bf)


def _final_kernel(x_ref, peer_ref, g2_ref, g_ref, o_ref):
    o_ref[...] = _rms(x_ref[...] + g2_ref[...] * peer_ref[...], g_ref[...])


def _final_norm(x1, peer, gate2, gain):
    seq, d = x1.shape
    tt = min(1024, seq)
    tok_spec = pl.BlockSpec((tt, d), lambda i: (i, 0))
    return pl.pallas_call(
        _final_kernel,
        grid=(seq // tt,),
        in_specs=[tok_spec, tok_spec, _full(gate2), _full(gain)],
        out_specs=tok_spec,
        out_shape=jax.ShapeDtypeStruct((seq, d), jnp.float32),
        name="final_norm",
    )(x1, peer, gate2, gain)


def _pack_table(tab):
    half = tab.shape[1] // 2
    bits = lax.bitcast_convert_type(tab.astype(jnp.bfloat16), jnp.uint16).astype(jnp.uint32)
    words = bits[:, :half] | (bits[:, half:] << 16)
    return lax.bitcast_convert_type(words, jnp.int32)


def _pack_rows(tab):
    words = _pack_table(tab)
    return words.reshape(words.shape[0], words.shape[1] // LANES, LANES)


def kernel(x, c, ada_w, ada_b, norm_mix, norm_ffn, w_in, gm_ws, gm_bs, gm_vnorm, out_norm_a, out_norm_b,
           w_out, peer_wq, peer_k1, peer_k2, peer_u, peer_v, final_norm):
    batch, seq, d = x.shape
    depth = ada_w.shape[0]
    bf = jnp.bfloat16
    mod = _ada_mod(c, ada_w, ada_b).reshape(depth, batch, 6, d)
    state = [(x[b], None, None) for b in range(batch)]
    for l in range(depth):
        row = lambda a: a[l].reshape(1, -1)
        w_in_bf, w_out_bf, wq_bf = w_in[l].astype(bf), w_out[l].astype(bf), peer_wq[l].astype(bf)
        gm_bst = gm_bs[l].T
        u_pack = _pack_rows(peer_u[l])
        v_pack = _pack_rows(peer_v[l])
        u_bf, v_bf = peer_u[l].astype(bf), peer_v[l].astype(bf)
        for b in range(batch):
            mod_b = mod[l, b]
            ya, q, k, v, xb = _mix_in(*state[b], mod_b, row(norm_mix), w_in_bf, gm_ws[l], gm_bst,
                                      row(gm_vnorm), row(out_norm_a))
            yb = _stick_break(q, k, v)
            x1, h2p, eid, gate = _mix_out(xb, ya, yb, mod_b, row(out_norm_b), w_out_bf, row(norm_ffn),
                                         wq_bf, peer_k1[l], peer_k2[l])
            if b >= batch - DENSE_SEQS:
                peer = _peer_dense(h2p, _gate_matrix(eid, gate, u_bf.shape[0]), u_bf, v_bf)
            else:
                peer = _peer_experts(u_pack, v_pack, eid, gate, h2p)
            state[b] = (x1, peer, mod_b[5:6])
    gain = final_norm.reshape(1, d)
    return jnp.stack([_final_norm(*st, gain) for st in state], axis=0)
```

```python
import functools

import jax
import jax.numpy as jnp
import numpy as np
from jax import lax
from jax.experimental import pallas as pl
from jax.experimental.pallas import tpu as pltpu
from jax.experimental.pallas import tpu_sc as plsc

EPS = 1e-6
CHUNK = 64
GM_HEADS = 8
GM_BLOCK = 128
HEAD_DIM = 64
PEER_HEADS = 8
PEER_HALF = 128
N_KEYS = 128
TOPK = 16
PICKS = PEER_HEADS * TOPK

LANES = 128
SUBLANES = 8
VMEM_LIMIT = 56 * 1024 * 1024

MIX_TOKENS = 512
SB_Q = 256
SB_K = 256
SB_DEAD = -120.0
OUT_TOKENS = 256
SC_ROWS = 64
SC_RING = 2
SC_COLS = 8
SC_DOT_ROWS = 8
DENSE_SEQS = 3
DENSE_TOKENS = 1024
DENSE_EXPERTS = 512


def _gelu(x):
    return 0.5 * x * (1.0 + jnp.tanh(np.sqrt(2.0 / np.pi) * (x + 0.044715 * (x * x * x))))


def _rms(x, gain):
    return x * lax.rsqrt(jnp.mean(x * x, axis=-1, keepdims=True) + EPS) * gain


def _full(a):
    return pl.BlockSpec(a.shape, lambda *_: (0,) * a.ndim)


def _mod_kernel(c_ref, w_ref, b_ref, o_ref):
    c = c_ref[...]
    c_act = c * jax.nn.sigmoid(c)
    o_ref[...] = jnp.dot(c_act, w_ref[...], preferred_element_type=jnp.float32) + b_ref[...]


def _ada_mod(c, ada_w, ada_b):
    depth, d, cols = ada_w.shape
    b = c.shape[0]
    tn = cols // 4
    return pl.pallas_call(
        _mod_kernel,
        grid=(depth, cols // tn),
        in_specs=[
            pl.BlockSpec((b, d), lambda l, j: (0, 0)),
            pl.BlockSpec((None, d, tn), lambda l, j: (l, 0, j)),
            pl.BlockSpec((None, 1, tn), lambda l, j: (l, 0, j)),
        ],
        out_specs=pl.BlockSpec((None, b, tn), lambda l, j: (l, 0, j)),
        out_shape=jax.ShapeDtypeStruct((depth, b, cols), jnp.float32),
        compiler_params=pltpu.CompilerParams(vmem_limit_bytes=VMEM_LIMIT),
        name="ada_mod",
    )(c, ada_w, ada_b.reshape(depth, 1, cols))


def _mix_in_kernel(has_peer, *refs):
    if has_peer:
        (x_ref, peer_ref, g2_ref, mod_ref, nm_ref, win_ref, ws_ref, bst_ref, vn_ref, na_ref,
         ya_ref, q_ref, k_ref, v_ref, xo_ref) = refs
        x = x_ref[...] + g2_ref[...] * peer_ref[...]
        xo_ref[...] = x
    else:
        (x_ref, mod_ref, nm_ref, win_ref, ws_ref, bst_ref, vn_ref, na_ref,
         ya_ref, q_ref, k_ref, v_ref) = refs
        x = x_ref[...]
    gw = GM_HEADS * HEAD_DIM
    mod = mod_ref[...]
    h = _rms(x, nm_ref[...]) * (1.0 + mod[1:2, :]) + mod[0:1, :]
    proj = jnp.dot(h.astype(jnp.bfloat16), win_ref[...], preferred_element_type=jnp.float32)
    u = _gelu(proj[:, 0:gw])
    v = _gelu(proj[:, gw:2 * gw])
    q_ref[...] = proj[:, 2 * gw:3 * gw].astype(jnp.bfloat16)
    k_ref[...] = proj[:, 3 * gw:4 * gw].astype(jnp.bfloat16)
    v_ref[...] = proj[:, 4 * gw:5 * gw].astype(jnp.bfloat16)

    r = lax.broadcasted_iota(jnp.int32, (gw, gw), 0) // HEAD_DIM
    cc = lax.broadcasted_iota(jnp.int32, (gw, gw), 1) // HEAD_DIM
    avg = jnp.where(r == cc, 1.0 / HEAD_DIM, 0.0).astype(jnp.bfloat16)
    ms = jnp.dot((v * v).astype(jnp.bfloat16), avg, preferred_element_type=jnp.float32)
    vh = (v * lax.rsqrt(ms + EPS) * vn_ref[...]).astype(jnp.bfloat16)

    t_chunk = lax.broadcasted_iota(jnp.int32, (GM_BLOCK, GM_BLOCK), 0) // CHUNK
    s_chunk = lax.broadcasted_iota(jnp.int32, (GM_BLOCK, GM_BLOCK), 1) // CHUNK
    keep = s_chunk <= t_chunk
    lane = lax.broadcasted_iota(jnp.int32, (GM_BLOCK, LANES), 1)
    first = lane < HEAD_DIM
    bst = bst_ref[...]
    n_blocks = x.shape[0] // GM_BLOCK
    for p in range(GM_HEADS // 2):
        w0 = jnp.where(keep, ws_ref[2 * p], 0.0).astype(jnp.bfloat16)
        w1 = jnp.where(keep, ws_ref[2 * p + 1], 0.0).astype(jnp.bfloat16)
        bias = jnp.where(first, bst[:, 2 * p:2 * p + 1], bst[:, 2 * p + 1:2 * p + 2])
        for nb in range(n_blocks):
            rows = slice(nb * GM_BLOCK, (nb + 1) * GM_BLOCK)
            cols = slice(p * LANES, (p + 1) * LANES)
            vp = vh[rows, cols]
            z0 = jnp.dot(w0, vp, preferred_element_type=jnp.float32)
            z1 = jnp.dot(w1, vp, preferred_element_type=jnp.float32)
            z = jnp.where(first, z0, z1) + bias
            ya_ref[rows, cols] = u[rows, cols] * z
    ya = ya_ref[...]
    ya_ref[...] = _rms(ya, na_ref[...])


def _mix_in(xb, peer, gate2, mod_b, norm_mix, w_in_bf, gm_ws, gm_bst, gm_vnorm, out_norm_a):
    seq, d = xb.shape
    gw = GM_HEADS * HEAD_DIM
    ts = min(MIX_TOKENS, seq)
    tok_spec = lambda w: pl.BlockSpec((ts, w), lambda i: (i, 0))
    has_peer = peer is not None
    weights = (mod_b, norm_mix, w_in_bf, gm_ws, gm_bst, gm_vnorm, out_norm_a)
    acts = (xb, peer, gate2) if has_peer else (xb,)
    act_specs = [tok_spec(d), tok_spec(d), _full(gate2)] if has_peer else [tok_spec(d)]
    out_specs = [tok_spec(gw)] * 4 + ([tok_spec(d)] if has_peer else [])
    out_shape = ([jax.ShapeDtypeStruct((seq, gw), jnp.float32)]
                 + [jax.ShapeDtypeStruct((seq, gw), jnp.bfloat16)] * 3
                 + ([jax.ShapeDtypeStruct((seq, d), jnp.float32)] if has_peer else []))
    outs = pl.pallas_call(
        functools.partial(_mix_in_kernel, has_peer),
        grid=(seq // ts,),
        in_specs=act_specs + [_full(w) for w in weights],
        out_specs=out_specs,
        out_shape=out_shape,
        compiler_params=pltpu.CompilerParams(vmem_limit_bytes=VMEM_LIMIT),
        name="mix_in",
    )(*acts, *weights)
    return tuple(outs) if has_peer else (*outs, xb)


def _sb_logs(qh, k, mask):
    z = lax.dot_general(qh, k, (((1,), (1,)), ((), ())), preferred_element_type=jnp.float32)
    soft = jnp.log(1.0 + jnp.exp(-jnp.abs(z)))
    log_beta = jnp.minimum(z, 0.0) - soft
    log_keep = log_beta - z
    if mask is not None:
        log_keep = jnp.where(mask, log_keep, 0.0)
    hi = log_keep.astype(jnp.bfloat16)
    lo = (log_keep - hi.astype(jnp.float32)).astype(jnp.bfloat16)
    return log_beta, log_keep, hi, lo


def _sb_later(hi, lo, tri):
    return (jnp.dot(hi, tri, preferred_element_type=jnp.float32)
            + jnp.dot(lo, tri, preferred_element_type=jnp.float32))


def _sb_step(qs, kvs, tri, carries, mask):
    logs = [[_sb_logs(qh, k, mask) for qh in qs] for k, _ in kvs]
    laters = [[_sb_later(lg[2], lg[3], tri) for lg in blk] for blk in logs]
    res = []
    for hh, carry in enumerate(carries):
        out = None
        for (_, v), blk, lat in zip(kvs, logs, laters):
            lg, later = blk[hh], lat[hh]
            a = jnp.exp(lg[0] + later + carry)
            if mask is not None:
                a = jnp.where(mask, a, 0.0)
            part = jnp.dot(a.astype(jnp.bfloat16), v, preferred_element_type=jnp.float32)
            out = part if out is None else out + part
            carry = carry + later[:, 0:1] + lg[1][:, 0:1]
        res.append((out, carry))
    return res


def _sb_kernel(q_ref, k_ref, v_ref, o_ref, acc_ref, car_ref):
    tq, tk = q_ref.shape[0], SB_K
    qi = pl.program_id(1)
    lane = lax.broadcasted_iota(jnp.int32, (tq, LANES), 1)
    first = lane < HEAD_DIM
    q = q_ref[...] * (1.0 / np.sqrt(HEAD_DIM))
    zero = jnp.zeros_like(q)
    qs = (jnp.where(first, q, zero), jnp.where(first, zero, q))
    tri = (lax.broadcasted_iota(jnp.int32, (tk, tk), 0)
           > lax.broadcasted_iota(jnp.int32, (tk, tk), 1)).astype(jnp.bfloat16)
    causal = (lax.broadcasted_iota(jnp.int32, (tq, tk), 1)
              < lax.broadcasted_iota(jnp.int32, (tq, tk), 0))

    def kv(block):
        off = pl.multiple_of(block * tk, tk)
        return k_ref[pl.ds(off, tk), :], v_ref[pl.ds(off, tk), :]

    def accumulate(res, assign=False):
        for hh in range(2):
            acc_ref[hh] = res[hh][0] if assign else acc_ref[hh] + res[hh][0]
            car_ref[hh] = jnp.broadcast_to(res[hh][1], (tq, LANES))

    def carries():
        return car_ref[0][:, 0:1], car_ref[1][:, 0:1]

    zero_carry = jnp.zeros((tq, 1), jnp.float32)
    accumulate(_sb_step(qs, [kv(qi)], tri, (zero_carry, zero_carry), causal), assign=True)

    def live():
        return (jnp.max(jnp.maximum(car_ref[0], car_ref[1])) > SB_DEAD).astype(jnp.int32)

    @pl.when(qi >= 1)
    def _():
        accumulate(_sb_step(qs, [kv(qi - 1)], tri, carries(), None))

    rest = jnp.maximum(qi - 1, 0)

    def cond(state):
        return jnp.logical_and(state[0] < rest // 2, state[1] > 0)

    def body(state):
        i = state[0]
        accumulate(_sb_step(qs, [kv(qi - 2 - 2 * i), kv(qi - 3 - 2 * i)], tri, carries(), None))
        return i + 1, live()

    _, still_live = lax.while_loop(cond, body, (jnp.int32(0), live()))

    @pl.when(jnp.logical_and(rest % 2 == 1, still_live > 0))
    def _():
        accumulate(_sb_step(qs, [kv(0)], tri, carries(), None))

    o_ref[...] = jnp.where(first, acc_ref[0], acc_ref[1])


def _stick_break(q, k, v):
    seq, w = q.shape
    pairs = w // LANES
    tq = min(SB_Q, seq)
    assert tq == min(SB_K, seq)
    q_spec = pl.BlockSpec((tq, LANES), lambda p, i: (i, p))
    kv_spec = pl.BlockSpec((seq, LANES), lambda p, i: (0, p))
    return pl.pallas_call(
        _sb_kernel,
        grid=(pairs, seq // tq),
        in_specs=[q_spec, kv_spec, kv_spec],
        out_specs=pl.BlockSpec((tq, LANES), lambda p, i: (i, p)),
        out_shape=jax.ShapeDtypeStruct((seq, w), jnp.float32),
        scratch_shapes=[pltpu.VMEM((2, tq, LANES), jnp.float32),
                        pltpu.VMEM((2, tq, LANES), jnp.float32)],
        compiler_params=pltpu.CompilerParams(vmem_limit_bytes=VMEM_LIMIT),
        name="stick_break",
    )(q, k, v)


def _top_rows(problems, k):
    t = problems[0][0].shape[1]
    slot = lax.broadcasted_iota(jnp.int32, (k, t), 0)
    ss = [s for s, _ in problems]
    labels = [label for _, label in problems]
    vals = [jnp.zeros((k, t), jnp.float32) for _ in problems]
    labs = [jnp.zeros((k, t), jnp.float32) for _ in problems]
    for r in range(k):
        ms = [jnp.max(s, axis=0, keepdims=True) for s in ss]
        ls = [jnp.min(jnp.where(s == m, label, jnp.inf), axis=0, keepdims=True)
              for s, m, label in zip(ss, ms, labels)]
        vals = [jnp.where(slot == r, m, v) for m, v in zip(ms, vals)]
        labs = [jnp.where(slot == r, lab, v) for lab, v in zip(ls, labs)]
        ss = [jnp.where(label == lab, -jnp.inf, s) for s, lab, label in zip(ss, ls, labels)]
    return list(zip(vals, labs))


def _pair_candidates(v1, v2):
    t = v1.shape[1]
    sub = lax.broadcasted_iota(jnp.int32, (SUBLANES, t), 0)
    sub_f = sub.astype(jnp.float32)
    sums, poss = [], []
    for i in range(SUBLANES):
        reach = TOPK // (i + 1)
        for j0 in range(0, reach, SUBLANES):
            piece = v1[i:i + 1, :] + v2[j0:j0 + SUBLANES, :]
            if reach - j0 < SUBLANES:
                piece = jnp.where(sub < reach - j0, piece, -jnp.inf)
            sums.append(piece)
            poss.append(sub_f + float(i * TOPK + j0))
    sums.append(v1[SUBLANES:TOPK, :] + v2[0:1, :])
    poss.append((sub_f + float(SUBLANES)) * float(TOPK))
    return jnp.concatenate(sums, axis=0), jnp.concatenate(poss, axis=0)


def _mix_out_kernel(x_ref, ya_ref, yb_ref, mod_ref, nb_ref, wout_ref, nf_ref, wq_ref, k1_ref, k2_ref,
                    x1_ref, h2_ref, eid_ref, gate_ref, q_scr, eid_scr, gate_scr):
    gw = ya_ref.shape[1]
    mod = mod_ref[...]
    ya = ya_ref[...].astype(jnp.bfloat16)
    yb = _rms(yb_ref[...], nb_ref[...]).astype(jnp.bfloat16)
    y = (jnp.dot(ya, wout_ref[0:gw, :], preferred_element_type=jnp.float32)
         + jnp.dot(yb, wout_ref[gw:2 * gw, :], preferred_element_type=jnp.float32))
    x1 = x_ref[...] + mod[2:3, :] * y
    x1_ref[...] = x1
    h2 = (_rms(x1, nf_ref[...]) * (1.0 + mod[4:5, :]) + mod[3:4, :]).astype(jnp.bfloat16)
    q_scr[...] = jnp.dot(h2, wq_ref[...], preferred_element_type=jnp.float32)
    half = h2.shape[1] // 2
    bits = lax.bitcast_convert_type(h2.astype(jnp.float32), jnp.int32)
    h2_ref[...] = jnp.bitwise_or(jnp.bitwise_and(bits[:, half:], jnp.int32(-65536)),
                                 lax.shift_right_logical(bits[:, :half], 16))

    k1 = k1_ref[...].astype(jnp.bfloat16)
    k2 = k2_ref[...].astype(jnp.bfloat16)
    t = LANES
    key_f = lax.broadcasted_iota(jnp.int32, (N_KEYS, t), 0).astype(jnp.float32)
    slot = lax.broadcasted_iota(jnp.int32, (TOPK, t), 0)
    slot_f = slot.astype(jnp.float32)
    nt = (((1,), (1,)), ((), ()))

    def sub_key_tops(h, r0):
        c0 = pl.multiple_of(h * 2 * PEER_HALF, 2 * PEER_HALF)
        qa = q_scr[pl.ds(r0, t), pl.ds(c0, PEER_HALF)].astype(jnp.bfloat16)
        qb = q_scr[pl.ds(r0, t), pl.ds(c0 + PEER_HALF, PEER_HALF)].astype(jnp.bfloat16)
        s1 = lax.dot_general(k1, qa, nt, preferred_element_type=jnp.float32)
        s2 = lax.dot_general(k2, qb, nt, preferred_element_type=jnp.float32)
        return _top_rows([(s1, key_f), (s2, key_f)], TOPK)

    def route(i, _):
        heads = (2 * (i % (PEER_HEADS // 2)), 2 * (i % (PEER_HEADS // 2)) + 1)
        r0 = pl.multiple_of((i // (PEER_HEADS // 2)) * t, t)
        halves = [sub_key_tops(h, r0) for h in heads]
        tops = _top_rows([_pair_candidates(hv[0][0], hv[1][0]) for hv in halves], TOPK)
        for h, ((_, i1), (_, i2)), (top_s, pos) in zip(heads, halves, tops):
            eid = jnp.zeros((TOPK, t), jnp.float32)
            for r in range(TOPK):
                pr = pos[r:r + 1, :]
                pi = jnp.floor(pr * (1.0 / TOPK))
                pj = pr - pi * TOPK
                e1 = jnp.sum(jnp.where(slot_f == pi, i1, 0.0), axis=0, keepdims=True)
                e2 = jnp.sum(jnp.where(slot_f == pj, i2, 0.0), axis=0, keepdims=True)
                eid = jnp.where(slot == r, e1 * N_KEYS + e2, eid)
            ex = jnp.exp(top_s - top_s[0:1, :])
            gate = ex / jnp.sum(ex, axis=0, keepdims=True)
            p0 = pl.multiple_of(h * TOPK, TOPK)
            eid_scr[pl.ds(p0, TOPK), pl.ds(r0, t)] = eid
            gate_scr[pl.ds(p0, TOPK), pl.ds(r0, t)] = gate
        return 0

    lax.fori_loop(0, (PEER_HEADS // 2) * (x1.shape[0] // t), route, 0)
    eid_ref[...] = eid_scr[...].T.astype(jnp.int32)
    gate_ref[...] = gate_scr[...].T


def _mix_out(xb, ya, yb, mod_b, out_norm_b, w_out_bf, norm_ffn, wq_bf, k1, k2):
    seq, d = xb.shape
    gw = ya.shape[1]
    tt = min(OUT_TOKENS, seq)
    qcols = wq_bf.shape[1]
    tok_spec = lambda w: pl.BlockSpec((tt, w), lambda i: (i, 0))
    return pl.pallas_call(
        _mix_out_kernel,
        grid=(seq // tt,),
        in_specs=[tok_spec(d), tok_spec(gw), tok_spec(gw), _full(mod_b), _full(out_norm_b), _full(w_out_bf),
                  _full(norm_ffn), _full(wq_bf), _full(k1), _full(k2)],
        out_specs=[tok_spec(d), tok_spec(d // 2), tok_spec(PICKS), tok_spec(PICKS)],
        out_shape=[
            jax.ShapeDtypeStruct((seq, d), jnp.float32),
            jax.ShapeDtypeStruct((seq, d // 2), jnp.int32),
            jax.ShapeDtypeStruct((seq, PICKS), jnp.int32),
            jax.ShapeDtypeStruct((seq, PICKS), jnp.float32),
        ],
        scratch_shapes=[pltpu.VMEM((tt, qcols), jnp.float32), pltpu.VMEM((PICKS, tt), jnp.float32),
                        pltpu.VMEM((PICKS, tt), jnp.float32)],
        compiler_params=pltpu.CompilerParams(vmem_limit_bytes=VMEM_LIMIT),
        name="mix_out",
    )(xb, ya, yb, mod_b, out_norm_b, w_out_bf, norm_ffn, wq_bf, k1, k2)


_PAIRS = plsc.PackFormat.INTERLEAVED


def _sum4_bf16(p):
    return plsc.unpack((p[0] + p[1]) + (p[2] + p[3]), format=_PAIRS)


def _peer_experts(u_tab, v_tab, idx, gates, h2p):
    seq, width = h2p.shape
    d = 2 * width
    bf = jnp.bfloat16
    info = plsc.get_sparse_core_info()
    lanes = info.num_lanes
    workers = info.num_cores * info.num_subcores
    tokens = seq // workers
    assert tokens * workers == seq and tokens % 2 == 0 and PICKS == SC_RING * SC_ROWS
    assert u_tab.shape[1:] == (width // LANES, LANES)
    kchunks = width // lanes
    mesh = plsc.VectorSubcoreMesh(core_axis_name="c", subcore_axis_name="s")
    rows_t = pltpu.VMEM((SC_ROWS, width // LANES, LANES), jnp.int32)
    planes_per_k4 = LANES // (4 * lanes)
    words_t = pltpu.VMEM((width,), jnp.int32)
    vec = lambda m: pltpu.VMEM((m,), jnp.float32)
    sem = pltpu.SemaphoreType.DMA
    c_gelu = float(np.sqrt(2.0 / np.pi))

    @functools.partial(
        pl.kernel, mesh=mesh,
        compiler_params=pltpu.CompilerParams(needs_layout_passes=False),
        out_type=jax.ShapeDtypeStruct((seq, d), jnp.float32),
        scratch_types=[pltpu.VMEM((tokens, PICKS), jnp.int32)] + [rows_t] * SC_RING
                      + [words_t, words_t, vec(PICKS), vec(PICKS), vec(d), vec(d), vec(PICKS), vec(PICKS)]
                      + [sem] * (SC_RING + 6),
    )
    def experts(u_hbm, v_hbm, idx_hbm, gate_hbm, h2_hbm, out_hbm, idx_v, *scratch):
        ring, scratch = scratch[:SC_RING], scratch[SC_RING:]
        x0, x1, gt0, gt1, o0, o1, act_v, w_v = scratch[:8]
        gsems, (sx0, sx1, sg0, sg1, so0, so1) = scratch[8:8 + SC_RING], scratch[8 + SC_RING:]
        wid = lax.axis_index("s") * info.num_cores + lax.axis_index("c")
        tok0 = wid * tokens
        pltpu.sync_copy(idx_hbm.at[wid], idx_v)
        lane = lax.iota(jnp.int32, lanes)
        xs, gts, outs = (x0, x1), (gt0, gt1), (o0, o1)
        sxs, sgs, sos = (sx0, sx1), (sg0, sg1), (so0, so1)

        def fetch(tab, i, rows, s):
            picks = idx_v.at[i // SC_RING, pl.ds((i % SC_RING) * SC_ROWS, SC_ROWS)]
            return pltpu.make_async_copy(tab.at[picks], rows, s)

        def xcopy(tk, par):
            return pltpu.make_async_copy(h2_hbm.at[tok0 + tk], xs[par], sxs[par])

        def gcopy(tk, par):
            return pltpu.make_async_copy(gate_hbm.at[tok0 + tk], gts[par], sgs[par])

        def ocopy(tk, par):
            return pltpu.make_async_copy(outs[par], out_hbm.at[tok0 + tk], sos[par])

        def dots(rows, xv, aoff):
            @pl.loop(0, SC_ROWS, step=lanes)
            def _(g):
                res = jnp.zeros((lanes,), jnp.float32)
                for rb in range(0, lanes, SC_DOT_ROWS):
                    def body(k4, accs):
                        off = pl.multiple_of(k4 * (4 * lanes), 4 * lanes)
                        xb = [plsc.bitcast(xv[pl.ds(off + c * lanes, lanes)], bf) for c in range(4)]
                        new = []
                        for r in range(SC_DOT_ROWS):
                            lo, hi = _sum4_bf16([
                                plsc.bitcast(rows[g + rb + r, k4 // planes_per_k4,
                                                  pl.ds((k4 % planes_per_k4) * (4 * lanes) + c * lanes, lanes)],
                                             bf) * xb[c]
                                for c in range(4)])
                            new.append(accs[r] + (lo + hi))
                        return tuple(new)
                    zeros = tuple(jnp.zeros((lanes,), jnp.float32) for _ in range(SC_DOT_ROWS))
                    accs = lax.fori_loop(0, kchunks // 4, body, zeros)
                    for r in range(SC_DOT_ROWS):
                        res = jnp.where(lane == rb + r, jnp.sum(accs[r]), res)
                act_v[pl.ds(pl.multiple_of(aoff + g, lanes), lanes)] = res

        def weights(gv):
            for c in range(PICKS // lanes):
                a = act_v[pl.ds(c * lanes, lanes)]
                y = c_gelu * (a + 0.044715 * (a * a * a))
                th = 1.0 - 2.0 / (1.0 + jnp.exp(2.0 * y))
                w_v[pl.ds(c * lanes, lanes)] = gv[pl.ds(c * lanes, lanes)] * (0.5 * a * (1.0 + th))

        def combine(rows, woff, ov, first):
            for cc in range(0, kchunks, SC_COLS):
                def body(j4, accs):
                    j = j4 * 4
                    wb = []
                    for r in range(4):
                        wj = plsc.load_gather(w_v, [jnp.full((lanes,), woff, jnp.int32) + (j + r)])
                        wb.append(plsc.pack(wj, wj, format=_PAIRS))
                    new = []
                    for c in range(SC_COLS):
                        lo, hi = _sum4_bf16([
                            plsc.bitcast(rows[j + r, (cc + c) * lanes // LANES,
                                              pl.ds((cc + c) * lanes % LANES, lanes)], bf) * wb[r]
                            for r in range(4)])
                        new.append(accs[2 * c] + lo)
                        new.append(accs[2 * c + 1] + hi)
                    return tuple(new)
                zeros = tuple(jnp.zeros((lanes,), jnp.float32) for _ in range(2 * SC_COLS))
                accs = lax.fori_loop(0, SC_ROWS // 4, body, zeros)
                for c in range(SC_COLS):
                    lo_sl = pl.ds((cc + c) * lanes, lanes)
                    hi_sl = pl.ds(width + (cc + c) * lanes, lanes)
                    if first:
                        ov[lo_sl] = accs[2 * c]
                        ov[hi_sl] = accs[2 * c + 1]
                    else:
                        ov[lo_sl] = ov[lo_sl] + accs[2 * c]
                        ov[hi_sl] = ov[hi_sl] + accs[2 * c + 1]

        def token(tk, par):
            xv, gv, ov = xs[par], gts[par], outs[par]
            for s in range(2 * SC_RING):
                q = s % SC_RING
                rows, gsem = ring[q], gsems[q]
                if s < SC_RING:
                    fetch(u_hbm, SC_RING * tk + q, rows, gsem).wait()
                    if s == 0:
                        xcopy(tk, par).wait()
                    dots(rows, xv, q * SC_ROWS)
                    fetch(v_hbm, SC_RING * tk + q, rows, gsem).start()
                    if s == SC_RING - 1:
                        gcopy(tk, par).wait()
                        weights(gv)
                else:
                    fetch(v_hbm, SC_RING * tk + q, rows, gsem).wait()
                    if s == SC_RING:
                        @pl.when(tk >= 2)
                        def _():
                            ocopy(tk - 2, par).wait()
                    combine(rows, q * SC_ROWS, ov, s == SC_RING)

                    @pl.when(tk + 1 < tokens)
                    def _():
                        fetch(u_hbm, SC_RING * (tk + 1) + q, rows, gsem).start()
                        if s == SC_RING:
                            xcopy(tk + 1, 1 - par).start()
                            gcopy(tk + 1, 1 - par).start()
            ocopy(tk, par).start()

        for q in range(SC_RING):
            fetch(u_hbm, q, ring[q], gsems[q]).start()
        xcopy(0, 0).start()
        gcopy(0, 0).start()

        @pl.loop(0, tokens, step=2)
        def _(tk):
            token(tk, 0)
            token(tk + 1, 1)

        ocopy(tokens - 2, 0).wait()
        ocopy(tokens - 1, 1).wait()

    return experts(u_tab, v_tab, idx.reshape(workers, tokens, PICKS), gates, h2p)


def _gate_matrix(idx, gates, n_experts):
    seq = idx.shape[0]
    info = plsc.get_sparse_core_info()
    lanes = info.num_lanes
    workers = info.num_cores * info.num_subcores
    tokens = seq // workers
    assert tokens * workers == seq and tokens % 2 == 0 and n_experts % lanes == 0
    mesh = plsc.VectorSubcoreMesh(core_axis_name="c", subcore_axis_name="s")
    row_t = pltpu.VMEM((n_experts,), jnp.float32)
    sem = pltpu.SemaphoreType.DMA

    @functools.partial(
        pl.kernel, mesh=mesh,
        compiler_params=pltpu.CompilerParams(needs_layout_passes=False),
        out_type=jax.ShapeDtypeStruct((seq, n_experts), jnp.float32),
        scratch_types=[pltpu.VMEM((tokens, PICKS), jnp.int32), pltpu.VMEM((tokens, PICKS), jnp.float32),
                       row_t, row_t, sem, sem],
    )
    def build(idx_hbm, gate_hbm, out_hbm, idx_v, gate_v, row0, row1, s0, s1):
        wid = lax.axis_index("s") * info.num_cores + lax.axis_index("c")
        tok0 = wid * tokens
        pltpu.sync_copy(idx_hbm.at[wid], idx_v)
        pltpu.sync_copy(gate_hbm.at[wid], gate_v)
        rows, sems = (row0, row1), (s0, s1)
        zeros = jnp.zeros((lanes,), jnp.float32)

        @pl.loop(0, n_experts, step=lanes)
        def _(i):
            row0[pl.ds(i, lanes)] = zeros
            row1[pl.ds(i, lanes)] = zeros

        def ocopy(tk, par):
            return pltpu.make_async_copy(rows[par], out_hbm.at[tok0 + tk], sems[par])

        def token(tk, par):
            row = rows[par]

            @pl.when(tk >= 2)
            def _():
                ocopy(tk - 2, par).wait()
                for c in range(PICKS // lanes):
                    plsc.store_scatter(row, [idx_v[tk - 2, pl.ds(c * lanes, lanes)]], zeros)
            for c in range(PICKS // lanes):
                sl = pl.ds(c * lanes, lanes)
                plsc.addupdate_scatter(row, [idx_v[tk, sl]], gate_v[tk, sl])
            ocopy(tk, par).start()

        @pl.loop(0, tokens, step=2)
        def _(tk):
            token(tk, 0)
            token(tk + 1, 1)

        ocopy(tokens - 2, 0).wait()
        ocopy(tokens - 1, 1).wait()

    shape = (workers, tokens, PICKS)
    return build(idx.reshape(shape), gates.reshape(shape))


def _dense_kernel(h2p_ref, g_ref, u_ref, v_ref, o_ref, h2_scr):
    half = h2p_ref.shape[1]

    @pl.when(pl.program_id(1) == 0)
    def _():
        words = h2p_ref[...]
        lo = lax.bitcast_convert_type(lax.shift_left(words, 16), jnp.float32)
        hi = lax.bitcast_convert_type(jnp.bitwise_and(words, jnp.int32(-65536)), jnp.float32)
        h2_scr[:, 0:half] = lo.astype(jnp.bfloat16)
        h2_scr[:, half:2 * half] = hi.astype(jnp.bfloat16)
        o_ref[...] = jnp.zeros_like(o_ref)

    s = lax.dot_general(h2_scr[...], u_ref[...], (((1,), (1,)), ((), ())),
                        preferred_element_type=jnp.float32)
    w = (g_ref[...] * _gelu(s)).astype(jnp.bfloat16)
    o_ref[...] += jnp.dot(w, v_ref[...], preferred_element_type=jnp.float32)


def _peer_dense(h2p, gmat, u_bf, v_bf):
    seq, half = h2p.shape
    n_experts, d = u_bf.shape
    tt = min(DENSE_TOKENS, seq)
    te = min(DENSE_EXPERTS, n_experts)
    return pl.pallas_call(
        _dense_kernel,
        grid=(seq // tt, n_experts // te),
        in_specs=[pl.BlockSpec((tt, half), lambda i, e: (i, 0)),
                  pl.BlockSpec((tt, te), lambda i, e: (i, e)),
                  pl.BlockSpec((te, d), lambda i, e: (e, 0)),
                  pl.BlockSpec((te, d), lambda i, e: (e, 0))],
        out_specs=pl.BlockSpec((tt, d), lambda i, e: (i, 0)),
        out_shape=jax.ShapeDtypeStruct((seq, d), jnp.float32),
        scratch_shapes=[pltpu.VMEM((tt, d), jnp.bfloat16)],
        compiler_params=pltpu.CompilerParams(vmem_limit_bytes=VMEM_LIMIT),
        name="peer_dense",
    )(h2p, gmat, u_bf, v_bf)


def _final_kernel(x_ref, peer_ref, g2_ref, g_ref, o_ref):
    o_ref[...] = _rms(x_ref[...] + g2_ref[...] * peer_ref[...], g_ref[...])


def _final_norm(x1, peer, gate2, gain):
    seq, d = x1.shape
    tt = min(1024, seq)
    tok_spec = pl.BlockSpec((tt, d), lambda i: (i, 0))
    return pl.pallas_call(
        _final_kernel,
        grid=(seq // tt,),
        in_specs=[tok_spec, tok_spec, _full(gate2), _full(gain)],
        out_specs=tok_spec,
        out_shape=jax.ShapeDtypeStruct((seq, d), jnp.float32),
        name="final_norm",
    )(x1, peer, gate2, gain)


def _pack_table(tab):
    half = tab.shape[1] // 2
    bits = lax.bitcast_convert_type(tab.astype(jnp.bfloat16), jnp.uint16).astype(jnp.uint32)
    words = bits[:, :half] | (bits[:, half:] << 16)
    return lax.bitcast_convert_type(words, jnp.int32)


def _pack_rows(tab):
    words = _pack_table(tab)
    return words.reshape(words.shape[0], words.shape[1] // LANES, LANES)


def kernel(x, c, ada_w, ada_b, norm_mix, norm_ffn, w_in, gm_ws, gm_bs, gm_vnorm, out_norm_a, out_norm_b,
           w_out, peer_wq, peer_k1, peer_k2, peer_u, peer_v, final_norm):
    batch, seq, d = x.shape
    depth = ada_w.shape[0]
    bf = jnp.bfloat16
    mod = _ada_mod(c, ada_w, ada_b).reshape(depth, batch, 6, d)
    state = [(x[b], None, None) for b in range(batch)]
    for l in range(depth):
        row = lambda a: a[l].reshape(1, -1)
        w_in_bf, w_out_bf, wq_bf = w_in[l].astype(bf), w_out[l].astype(bf), peer_wq[l].astype(bf)
        gm_bst = gm_bs[l].T
        u_pack = _pack_rows(peer_u[l])
        v_pack = _pack_rows(peer_v[l])
        u_bf, v_bf = peer_u[l].astype(bf), peer_v[l].astype(bf)
        for b in range(batch):
            mod_b = mod[l, b]
            ya, q, k, v, xb = _mix_in(*state[b], mod_b, row(norm_mix), w_in_bf, gm_ws[l], gm_bst,
                                      row(gm_vnorm), row(out_norm_a))
            yb = _stick_break(q, k, v)
            x1, h2p, eid, gate = _mix_out(xb, ya, yb, mod_b, row(out_norm_b), w_out_bf, row(norm_ffn),
                                         wq_bf, peer_k1[l], peer_k2[l])
            if b >= batch - DENSE_SEQS:
                peer = _peer_dense(h2p, _gate_matrix(eid, gate, u_bf.shape[0]), u_bf, v_bf)
            else:
                peer = _peer_experts(u_pack, v_pack, eid, gate, h2p)
            state[b] = (x1, peer, mod_b[5:6])
    gain = final_norm.reshape(1, d)
    return jnp.stack([_final_norm(*st, gain) for st in state], axis=0)
```

```python
import functools

import jax
import jax.numpy as jnp
import numpy as np
from jax import lax
from jax.experimental import pallas as pl
from jax.experimental.pallas import tpu as pltpu
from jax.experimental.pallas import tpu_sc as plsc

EPS = 1e-6
CHUNK = 64
GM_HEADS = 8
GM_BLOCK = 128
HEAD_DIM = 64
PEER_HEADS = 8
PEER_HALF = 128
N_KEYS = 128
TOPK = 16
PICKS = PEER_HEADS * TOPK

LANES = 128
SUBLANES = 8
VMEM_LIMIT = 56 * 1024 * 1024

MIX_TOKENS = 512
SB_Q = 256
SB_K = 256
SB_DEAD = -120.0
OUT_TOKENS = 256
SC_ROWS = 64
SC_RING = 2
SC_COLS = 8
SC_DOT_ROWS = 8
DENSE_SEQS = (4, 3)
DENSE_TOKENS = 1024
DENSE_EXPERTS = 512


def _gelu(x):
    return 0.5 * x * (1.0 + jnp.tanh(np.sqrt(2.0 / np.pi) * (x + 0.044715 * (x * x * x))))


def _rms(x, gain):
    return x * lax.rsqrt(jnp.mean(x * x, axis=-1, keepdims=True) + EPS) * gain


def _full(a):
    return pl.BlockSpec(a.shape, lambda *_: (0,) * a.ndim)


def _mod_kernel(c_ref, w_ref, b_ref, o_ref):
    c = c_ref[...]
    c_act = c * jax.nn.sigmoid(c)
    o_ref[...] = jnp.dot(c_act, w_ref[...], preferred_element_type=jnp.float32) + b_ref[...]


def _ada_mod(c, ada_w, ada_b):
    depth, d, cols = ada_w.shape
    b = c.shape[0]
    tn = cols // 4
    return pl.pallas_call(
        _mod_kernel,
        grid=(depth, cols // tn),
        in_specs=[
            pl.BlockSpec((b, d), lambda l, j: (0, 0)),
            pl.BlockSpec((None, d, tn), lambda l, j: (l, 0, j)),
            pl.BlockSpec((None, 1, tn), lambda l, j: (l, 0, j)),
        ],
        out_specs=pl.BlockSpec((None, b, tn), lambda l, j: (l, 0, j)),
        out_shape=jax.ShapeDtypeStruct((depth, b, cols), jnp.float32),
        compiler_params=pltpu.CompilerParams(vmem_limit_bytes=VMEM_LIMIT),
        name="ada_mod",
    )(c, ada_w, ada_b.reshape(depth, 1, cols))


def _mix_in_kernel(has_peer, *refs):
    if has_peer:
        (x_ref, peer_ref, g2_ref, mod_ref, nm_ref, win_ref, ws_ref, bst_ref, vn_ref, na_ref,
         ya_ref, q_ref, k_ref, v_ref, xo_ref) = refs
        x = x_ref[...] + g2_ref[...] * peer_ref[...]
        xo_ref[...] = x
    else:
        (x_ref, mod_ref, nm_ref, win_ref, ws_ref, bst_ref, vn_ref, na_ref,
         ya_ref, q_ref, k_ref, v_ref) = refs
        x = x_ref[...]
    gw = GM_HEADS * HEAD_DIM
    mod = mod_ref[...]
    h = _rms(x, nm_ref[...]) * (1.0 + mod[1:2, :]) + mod[0:1, :]
    proj = jnp.dot(h.astype(jnp.bfloat16), win_ref[...], preferred_element_type=jnp.float32)
    u = _gelu(proj[:, 0:gw])
    v = _gelu(proj[:, gw:2 * gw])
    q_ref[...] = proj[:, 2 * gw:3 * gw].astype(jnp.bfloat16)
    k_ref[...] = proj[:, 3 * gw:4 * gw].astype(jnp.bfloat16)
    v_ref[...] = proj[:, 4 * gw:5 * gw].astype(jnp.bfloat16)

    r = lax.broadcasted_iota(jnp.int32, (gw, gw), 0) // HEAD_DIM
    cc = lax.broadcasted_iota(jnp.int32, (gw, gw), 1) // HEAD_DIM
    avg = jnp.where(r == cc, 1.0 / HEAD_DIM, 0.0).astype(jnp.bfloat16)
    ms = jnp.dot((v * v).astype(jnp.bfloat16), avg, preferred_element_type=jnp.float32)
    vh = (v * lax.rsqrt(ms + EPS) * vn_ref[...]).astype(jnp.bfloat16)

    t_chunk = lax.broadcasted_iota(jnp.int32, (GM_BLOCK, GM_BLOCK), 0) // CHUNK
    s_chunk = lax.broadcasted_iota(jnp.int32, (GM_BLOCK, GM_BLOCK), 1) // CHUNK
    keep = s_chunk <= t_chunk
    lane = lax.broadcasted_iota(jnp.int32, (GM_BLOCK, LANES), 1)
    first = lane < HEAD_DIM
    bst = bst_ref[...]
    n_blocks = x.shape[0] // GM_BLOCK
    for p in range(GM_HEADS // 2):
        w0 = jnp.where(keep, ws_ref[2 * p], 0.0).astype(jnp.bfloat16)
        w1 = jnp.where(keep, ws_ref[2 * p + 1], 0.0).astype(jnp.bfloat16)
        bias = jnp.where(first, bst[:, 2 * p:2 * p + 1], bst[:, 2 * p + 1:2 * p + 2])
        for nb in range(n_blocks):
            rows = slice(nb * GM_BLOCK, (nb + 1) * GM_BLOCK)
            cols = slice(p * LANES, (p + 1) * LANES)
            vp = vh[rows, cols]
            z0 = jnp.dot(w0, vp, preferred_element_type=jnp.float32)
            z1 = jnp.dot(w1, vp, preferred_element_type=jnp.float32)
            z = jnp.where(first, z0, z1) + bias
            ya_ref[rows, cols] = u[rows, cols] * z
    ya = ya_ref[...]
    ya_ref[...] = _rms(ya, na_ref[...])


def _mix_in(xb, peer, gate2, mod_b, norm_mix, w_in_bf, gm_ws, gm_bst, gm_vnorm, out_norm_a):
    seq, d = xb.shape
    gw = GM_HEADS * HEAD_DIM
    ts = min(MIX_TOKENS, seq)
    tok_spec = lambda w: pl.BlockSpec((ts, w), lambda i: (i, 0))
    has_peer = peer is not None
    weights = (mod_b, norm_mix, w_in_bf, gm_ws, gm_bst, gm_vnorm, out_norm_a)
    acts = (xb, peer, gate2) if has_peer else (xb,)
    act_specs = [tok_spec(d), tok_spec(d), _full(gate2)] if has_peer else [tok_spec(d)]
    out_specs = [tok_spec(gw)] * 4 + ([tok_spec(d)] if has_peer else [])
    out_shape = ([jax.ShapeDtypeStruct((seq, gw), jnp.float32)]
                 + [jax.ShapeDtypeStruct((seq, gw), jnp.bfloat16)] * 3
                 + ([jax.ShapeDtypeStruct((seq, d), jnp.float32)] if has_peer else []))
    outs = pl.pallas_call(
        functools.partial(_mix_in_kernel, has_peer),
        grid=(seq // ts,),
        in_specs=act_specs + [_full(w) for w in weights],
        out_specs=out_specs,
        out_shape=out_shape,
        compiler_params=pltpu.CompilerParams(vmem_limit_bytes=VMEM_LIMIT),
        name="mix_in",
    )(*acts, *weights)
    return tuple(outs) if has_peer else (*outs, xb)


def _sb_logs(qh, k, mask):
    z = lax.dot_general(qh, k, (((1,), (1,)), ((), ())), preferred_element_type=jnp.float32)
    soft = jnp.log(1.0 + jnp.exp(-jnp.abs(z)))
    log_beta = jnp.minimum(z, 0.0) - soft
    log_keep = log_beta - z
    if mask is not None:
        log_keep = jnp.where(mask, log_keep, 0.0)
    hi = log_keep.astype(jnp.bfloat16)
    lo = (log_keep - hi.astype(jnp.float32)).astype(jnp.bfloat16)
    return log_beta, log_keep, hi, lo


def _sb_later(hi, lo, tri):
    return (jnp.dot(hi, tri, preferred_element_type=jnp.float32)
            + jnp.dot(lo, tri, preferred_element_type=jnp.float32))


def _sb_step(qs, kvs, tri, carries, mask):
    logs = [[_sb_logs(qh, k, mask) for qh in qs] for k, _ in kvs]
    laters = [[_sb_later(lg[2], lg[3], tri) for lg in blk] for blk in logs]
    res = []
    for hh, carry in enumerate(carries):
        out = None
        for (_, v), blk, lat in zip(kvs, logs, laters):
            lg, later = blk[hh], lat[hh]
            a = jnp.exp(lg[0] + later + carry)
            if mask is not None:
                a = jnp.where(mask, a, 0.0)
            part = jnp.dot(a.astype(jnp.bfloat16), v, preferred_element_type=jnp.float32)
            out = part if out is None else out + part
            carry = carry + later[:, 0:1] + lg[1][:, 0:1]
        res.append((out, carry))
    return res


def _sb_kernel(q_ref, k_ref, v_ref, o_ref, acc_ref, car_ref):
    tq, tk = q_ref.shape[0], SB_K
    qi = pl.program_id(1)
    lane = lax.broadcasted_iota(jnp.int32, (tq, LANES), 1)
    first = lane < HEAD_DIM
    q = q_ref[...] * (1.0 / np.sqrt(HEAD_DIM))
    zero = jnp.zeros_like(q)
    qs = (jnp.where(first, q, zero), jnp.where(first, zero, q))
    tri = (lax.broadcasted_iota(jnp.int32, (tk, tk), 0)
           > lax.broadcasted_iota(jnp.int32, (tk, tk), 1)).astype(jnp.bfloat16)
    causal = (lax.broadcasted_iota(jnp.int32, (tq, tk), 1)
              < lax.broadcasted_iota(jnp.int32, (tq, tk), 0))

    def kv(block):
        off = pl.multiple_of(block * tk, tk)
        return k_ref[pl.ds(off, tk), :], v_ref[pl.ds(off, tk), :]

    def accumulate(res, assign=False):
        for hh in range(2):
            acc_ref[hh] = res[hh][0] if assign else acc_ref[hh] + res[hh][0]
            car_ref[hh] = jnp.broadcast_to(res[hh][1], (tq, LANES))

    def carries():
        return car_ref[0][:, 0:1], car_ref[1][:, 0:1]

    zero_carry = jnp.zeros((tq, 1), jnp.float32)
    accumulate(_sb_step(qs, [kv(qi)], tri, (zero_carry, zero_carry), causal), assign=True)

    def live():
        return (jnp.max(jnp.maximum(car_ref[0], car_ref[1])) > SB_DEAD).astype(jnp.int32)

    @pl.when(qi >= 1)
    def _():
        accumulate(_sb_step(qs, [kv(qi - 1)], tri, carries(), None))

    rest = jnp.maximum(qi - 1, 0)

    def cond(state):
        return jnp.logical_and(state[0] < rest // 2, state[1] > 0)

    def body(state):
        i = state[0]
        accumulate(_sb_step(qs, [kv(qi - 2 - 2 * i), kv(qi - 3 - 2 * i)], tri, carries(), None))
        return i + 1, live()

    _, still_live = lax.while_loop(cond, body, (jnp.int32(0), live()))

    @pl.when(jnp.logical_and(rest % 2 == 1, still_live > 0))
    def _():
        accumulate(_sb_step(qs, [kv(0)], tri, carries(), None))

    o_ref[...] = jnp.where(first, acc_ref[0], acc_ref[1])


def _stick_break(q, k, v):
    seq, w = q.shape
    pairs = w // LANES
    tq = min(SB_Q, seq)
    assert tq == min(SB_K, seq)
    q_spec = pl.BlockSpec((tq, LANES), lambda p, i: (i, p))
    kv_spec = pl.BlockSpec((seq, LANES), lambda p, i: (0, p))
    return pl.pallas_call(
        _sb_kernel,
        grid=(pairs, seq // tq),
        in_specs=[q_spec, kv_spec, kv_spec],
        out_specs=pl.BlockSpec((tq, LANES), lambda p, i: (i, p)),
        out_shape=jax.ShapeDtypeStruct((seq, w), jnp.float32),
        scratch_shapes=[pltpu.VMEM((2, tq, LANES), jnp.float32),
                        pltpu.VMEM((2, tq, LANES), jnp.float32)],
        compiler_params=pltpu.CompilerParams(vmem_limit_bytes=VMEM_LIMIT),
        name="stick_break",
    )(q, k, v)


def _top_rows(problems, k):
    t = problems[0][0].shape[1]
    slot = lax.broadcasted_iota(jnp.int32, (k, t), 0)
    ss = [s for s, _ in problems]
    labels = [label for _, label in problems]
    vals = [jnp.zeros((k, t), jnp.float32) for _ in problems]
    labs = [jnp.zeros((k, t), jnp.float32) for _ in problems]
    for r in range(k):
        ms = [jnp.max(s, axis=0, keepdims=True) for s in ss]
        ls = [jnp.min(jnp.where(s == m, label, jnp.inf), axis=0, keepdims=True)
              for s, m, label in zip(ss, ms, labels)]
        vals = [jnp.where(slot == r, m, v) for m, v in zip(ms, vals)]
        labs = [jnp.where(slot == r, lab, v) for lab, v in zip(ls, labs)]
        ss = [jnp.where(label == lab, -jnp.inf, s) for s, lab, label in zip(ss, ls, labels)]
    return list(zip(vals, labs))


def _sort_network(n):
    pairs = []
    p = 1
    while p < n:
        k = p
        while k >= 1:
            for j in range(k % p, n - k, 2 * k):
                for i in range(min(k, n - j - k)):
                    if (i + j) // (2 * p) == (i + j + k) // (2 * p):
                        pairs.append((i + j, i + j + k))
            k //= 2
        p *= 2
    return pairs


def _top_keys(scores, k):
    c, t = scores[0].shape
    depth = c // SUBLANES
    net = _sort_network(depth)
    sub = lax.broadcasted_iota(jnp.int32, (SUBLANES, t), 0).astype(jnp.float32)
    slot = lax.broadcasted_iota(jnp.int32, (k, t), 0)
    cols = []
    for s in scores:
        vals = [s[r * SUBLANES:(r + 1) * SUBLANES, :] for r in range(depth)]
        rows = [sub + float(r * SUBLANES) for r in range(depth)]
        for a, b in net:
            swap = (vals[b] > vals[a]) | ((vals[b] == vals[a]) & (rows[b] < rows[a]))
            vals[a], vals[b] = jnp.where(swap, vals[b], vals[a]), jnp.where(swap, vals[a], vals[b])
            rows[a], rows[b] = jnp.where(swap, rows[b], rows[a]), jnp.where(swap, rows[a], rows[b])
        cols.append((vals, rows))
    outs = [(jnp.zeros((k, t), jnp.float32), jnp.zeros((k, t), jnp.float32)) for _ in scores]
    for r in range(k):
        heads = [(jnp.max(vals[0], axis=0, keepdims=True), vals[0], rows[0]) for vals, rows in cols]
        picks = [(m, jnp.min(jnp.where(v0 == m, r0, jnp.inf), axis=0, keepdims=True)) for m, v0, r0 in heads]
        outs = [(jnp.where(slot == r, m, tv), jnp.where(slot == r, row, tr))
                for (m, row), (tv, tr) in zip(picks, outs)]
        left = min(depth - 1, k - 1 - r)
        new_cols = []
        for (vals, rows), (_, row) in zip(cols, picks):
            sel = rows[0] == row
            tail = [jnp.where(sel, -jnp.inf, vals[-1])] if left == depth - 1 else []
            vals = [jnp.where(sel, vals[i + 1], vals[i]) for i in range(left)] + tail + vals[left + len(tail):]
            rows = [jnp.where(sel, rows[i + 1], rows[i]) for i in range(left)] + rows[left:]
            new_cols.append((vals, rows))
        cols = new_cols
    return outs


def _pair_candidates(v1, v2):
    t = v1.shape[1]
    sub = lax.broadcasted_iota(jnp.int32, (SUBLANES, t), 0)
    sub_f = sub.astype(jnp.float32)
    sums, poss = [], []
    for i in range(SUBLANES):
        reach = TOPK // (i + 1)
        for j0 in range(0, reach, SUBLANES):
            piece = v1[i:i + 1, :] + v2[j0:j0 + SUBLANES, :]
            if reach - j0 < SUBLANES:
                piece = jnp.where(sub < reach - j0, piece, -jnp.inf)
            sums.append(piece)
            poss.append(sub_f + float(i * TOPK + j0))
    sums.append(v1[SUBLANES:TOPK, :] + v2[0:1, :])
    poss.append((sub_f + float(SUBLANES)) * float(TOPK))
    return jnp.concatenate(sums, axis=0), jnp.concatenate(poss, axis=0)


def _mix_out_kernel(x_ref, ya_ref, yb_ref, mod_ref, nb_ref, wout_ref, nf_ref, wq_ref, k1_ref, k2_ref,
                    x1_ref, h2_ref, eid_ref, gate_ref, q_scr, eid_scr, gate_scr):
    gw = ya_ref.shape[1]
    mod = mod_ref[...]
    ya = ya_ref[...].astype(jnp.bfloat16)
    yb = _rms(yb_ref[...], nb_ref[...]).astype(jnp.bfloat16)
    y = (jnp.dot(ya, wout_ref[0:gw, :], preferred_element_type=jnp.float32)
         + jnp.dot(yb, wout_ref[gw:2 * gw, :], preferred_element_type=jnp.float32))
    x1 = x_ref[...] + mod[2:3, :] * y
    x1_ref[...] = x1
    h2 = (_rms(x1, nf_ref[...]) * (1.0 + mod[4:5, :]) + mod[3:4, :]).astype(jnp.bfloat16)
    q_scr[...] = jnp.dot(h2, wq_ref[...], preferred_element_type=jnp.float32)
    half = h2.shape[1] // 2
    bits = lax.bitcast_convert_type(h2.astype(jnp.float32), jnp.int32)
    h2_ref[...] = jnp.bitwise_or(jnp.bitwise_and(bits[:, half:], jnp.int32(-65536)),
                                 lax.shift_right_logical(bits[:, :half], 16))

    k1 = k1_ref[...].astype(jnp.bfloat16)
    k2 = k2_ref[...].astype(jnp.bfloat16)
    t = LANES
    slot = lax.broadcasted_iota(jnp.int32, (TOPK, t), 0)
    slot_f = slot.astype(jnp.float32)
    nt = (((1,), (1,)), ((), ()))

    def sub_key_tops(h, r0):
        c0 = pl.multiple_of(h * 2 * PEER_HALF, 2 * PEER_HALF)
        qa = q_scr[pl.ds(r0, t), pl.ds(c0, PEER_HALF)].astype(jnp.bfloat16)
        qb = q_scr[pl.ds(r0, t), pl.ds(c0 + PEER_HALF, PEER_HALF)].astype(jnp.bfloat16)
        s1 = lax.dot_general(k1, qa, nt, preferred_element_type=jnp.float32)
        s2 = lax.dot_general(k2, qb, nt, preferred_element_type=jnp.float32)
        return _top_keys([s1, s2], TOPK)

    def route(i, _):
        heads = (2 * (i % (PEER_HEADS // 2)), 2 * (i % (PEER_HEADS // 2)) + 1)
        r0 = pl.multiple_of((i // (PEER_HEADS // 2)) * t, t)
        halves = [sub_key_tops(h, r0) for h in heads]
        tops = _top_rows([_pair_candidates(hv[0][0], hv[1][0]) for hv in halves], TOPK)
        for h, ((_, i1), (_, i2)), (top_s, pos) in zip(heads, halves, tops):
            eid = jnp.zeros((TOPK, t), jnp.float32)
            for r in range(TOPK):
                pr = pos[r:r + 1, :]
                pi = jnp.floor(pr * (1.0 / TOPK))
                pj = pr - pi * TOPK
                e1 = jnp.sum(jnp.where(slot_f == pi, i1, 0.0), axis=0, keepdims=True)
                e2 = jnp.sum(jnp.where(slot_f == pj, i2, 0.0), axis=0, keepdims=True)
                eid = jnp.where(slot == r, e1 * N_KEYS + e2, eid)
            ex = jnp.exp(top_s - top_s[0:1, :])
            gate = ex / jnp.sum(ex, axis=0, keepdims=True)
            p0 = pl.multiple_of(h * TOPK, TOPK)
            eid_scr[pl.ds(p0, TOPK), pl.ds(r0, t)] = eid
            gate_scr[pl.ds(p0, TOPK), pl.ds(r0, t)] = gate
        return 0

    lax.fori_loop(0, (PEER_HEADS // 2) * (x1.shape[0] // t), route, 0)
    eid_ref[...] = eid_scr[...].T.astype(jnp.int32)
    gate_ref[...] = gate_scr[...].T


def _mix_out(xb, ya, yb, mod_b, out_norm_b, w_out_bf, norm_ffn, wq_bf, k1, k2):
    seq, d = xb.shape
    gw = ya.shape[1]
    tt = min(OUT_TOKENS, seq)
    qcols = wq_bf.shape[1]
    tok_spec = lambda w: pl.BlockSpec((tt, w), lambda i: (i, 0))
    return pl.pallas_call(
        _mix_out_kernel,
        grid=(seq // tt,),
        in_specs=[tok_spec(d), tok_spec(gw), tok_spec(gw), _full(mod_b), _full(out_norm_b), _full(w_out_bf),
                  _full(norm_ffn), _full(wq_bf), _full(k1), _full(k2)],
        out_specs=[tok_spec(d), tok_spec(d // 2), tok_spec(PICKS), tok_spec(PICKS)],
        out_shape=[
            jax.ShapeDtypeStruct((seq, d), jnp.float32),
            jax.ShapeDtypeStruct((seq, d // 2), jnp.int32),
            jax.ShapeDtypeStruct((seq, PICKS), jnp.int32),
            jax.ShapeDtypeStruct((seq, PICKS), jnp.float32),
        ],
        scratch_shapes=[pltpu.VMEM((tt, qcols), jnp.float32), pltpu.VMEM((PICKS, tt), jnp.float32),
                        pltpu.VMEM((PICKS, tt), jnp.float32)],
        compiler_params=pltpu.CompilerParams(vmem_limit_bytes=VMEM_LIMIT),
        name="mix_out",
    )(xb, ya, yb, mod_b, out_norm_b, w_out_bf, norm_ffn, wq_bf, k1, k2)


_PAIRS = plsc.PackFormat.INTERLEAVED


def _sum4_bf16(p):
    return plsc.unpack((p[0] + p[1]) + (p[2] + p[3]), format=_PAIRS)


def _peer_experts(u_tab, v_tab, idx, gates, h2p):
    seq, width = h2p.shape
    d = 2 * width
    bf = jnp.bfloat16
    info = plsc.get_sparse_core_info()
    lanes = info.num_lanes
    workers = info.num_cores * info.num_subcores
    tokens = seq // workers
    assert tokens * workers == seq and tokens % 2 == 0 and PICKS == SC_RING * SC_ROWS
    assert u_tab.shape[1:] == (width // LANES, LANES)
    kchunks = width // lanes
    mesh = plsc.VectorSubcoreMesh(core_axis_name="c", subcore_axis_name="s")
    rows_t = pltpu.VMEM((SC_ROWS, width // LANES, LANES), jnp.int32)
    planes_per_k4 = LANES // (4 * lanes)
    words_t = pltpu.VMEM((width,), jnp.int32)
    vec = lambda m: pltpu.VMEM((m,), jnp.float32)
    sem = pltpu.SemaphoreType.DMA
    c_gelu = float(np.sqrt(2.0 / np.pi))

    @functools.partial(
        pl.kernel, mesh=mesh,
        compiler_params=pltpu.CompilerParams(needs_layout_passes=False),
        out_type=jax.ShapeDtypeStruct((seq, d), jnp.float32),
        scratch_types=[pltpu.VMEM((tokens, PICKS), jnp.int32)] + [rows_t] * SC_RING
                      + [words_t, words_t, vec(PICKS), vec(PICKS), vec(d), vec(d), vec(PICKS), vec(PICKS)]
                      + [sem] * (SC_RING + 6),
    )
    def experts(u_hbm, v_hbm, idx_hbm, gate_hbm, h2_hbm, out_hbm, idx_v, *scratch):
        ring, scratch = scratch[:SC_RING], scratch[SC_RING:]
        x0, x1, gt0, gt1, o0, o1, act_v, w_v = scratch[:8]
        gsems, (sx0, sx1, sg0, sg1, so0, so1) = scratch[8:8 + SC_RING], scratch[8 + SC_RING:]
        wid = lax.axis_index("s") * info.num_cores + lax.axis_index("c")
        tok0 = wid * tokens
        pltpu.sync_copy(idx_hbm.at[wid], idx_v)
        lane = lax.iota(jnp.int32, lanes)
        xs, gts, outs = (x0, x1), (gt0, gt1), (o0, o1)
        sxs, sgs, sos = (sx0, sx1), (sg0, sg1), (so0, so1)

        def fetch(tab, i, rows, s):
            picks = idx_v.at[i // SC_RING, pl.ds((i % SC_RING) * SC_ROWS, SC_ROWS)]
            return pltpu.make_async_copy(tab.at[picks], rows, s)

        def xcopy(tk, par):
            return pltpu.make_async_copy(h2_hbm.at[tok0 + tk], xs[par], sxs[par])

        def gcopy(tk, par):
            return pltpu.make_async_copy(gate_hbm.at[tok0 + tk], gts[par], sgs[par])

        def ocopy(tk, par):
            return pltpu.make_async_copy(outs[par], out_hbm.at[tok0 + tk], sos[par])

        def dots(rows, xv, aoff):
            @pl.loop(0, SC_ROWS, step=lanes)
            def _(g):
                res = jnp.zeros((lanes,), jnp.float32)
                for rb in range(0, lanes, SC_DOT_ROWS):
                    def body(k4, accs):
                        off = pl.multiple_of(k4 * (4 * lanes), 4 * lanes)
                        xb = [plsc.bitcast(xv[pl.ds(off + c * lanes, lanes)], bf) for c in range(4)]
                        new = []
                        for r in range(SC_DOT_ROWS):
                            lo, hi = _sum4_bf16([
                                plsc.bitcast(rows[g + rb + r, k4 // planes_per_k4,
                                                  pl.ds((k4 % planes_per_k4) * (4 * lanes) + c * lanes, lanes)],
                                             bf) * xb[c]
                                for c in range(4)])
                            new.append(accs[r] + (lo + hi))
                        return tuple(new)
                    zeros = tuple(jnp.zeros((lanes,), jnp.float32) for _ in range(SC_DOT_ROWS))
                    accs = lax.fori_loop(0, kchunks // 4, body, zeros)
                    for r in range(SC_DOT_ROWS):
                        res = jnp.where(lane == rb + r, jnp.sum(accs[r]), res)
                act_v[pl.ds(pl.multiple_of(aoff + g, lanes), lanes)] = res

        def weights(gv):
            for c in range(PICKS // lanes):
                a = act_v[pl.ds(c * lanes, lanes)]
                y = c_gelu * (a + 0.044715 * (a * a * a))
                th = 1.0 - 2.0 / (1.0 + jnp.exp(2.0 * y))
                w_v[pl.ds(c * lanes, lanes)] = gv[pl.ds(c * lanes, lanes)] * (0.5 * a * (1.0 + th))

        def combine(rows, woff, ov, first):
            for cc in range(0, kchunks, SC_COLS):
                def body(j4, accs):
                    j = j4 * 4
                    wb = []
                    for r in range(4):
                        wj = plsc.load_gather(w_v, [jnp.full((lanes,), woff, jnp.int32) + (j + r)])
                        wb.append(plsc.pack(wj, wj, format=_PAIRS))
                    new = []
                    for c in range(SC_COLS):
                        lo, hi = _sum4_bf16([
                            plsc.bitcast(rows[j + r, (cc + c) * lanes // LANES,
                                              pl.ds((cc + c) * lanes % LANES, lanes)], bf) * wb[r]
                            for r in range(4)])
                        new.append(accs[2 * c] + lo)
                        new.append(accs[2 * c + 1] + hi)
                    return tuple(new)
                zeros = tuple(jnp.zeros((lanes,), jnp.float32) for _ in range(2 * SC_COLS))
                accs = lax.fori_loop(0, SC_ROWS // 4, body, zeros)
                for c in range(SC_COLS):
                    lo_sl = pl.ds((cc + c) * lanes, lanes)
                    hi_sl = pl.ds(width + (cc + c) * lanes, lanes)
                    if first:
                        ov[lo_sl] = accs[2 * c]
                        ov[hi_sl] = accs[2 * c + 1]
                    else:
                        ov[lo_sl] = ov[lo_sl] + accs[2 * c]
                        ov[hi_sl] = ov[hi_sl] + accs[2 * c + 1]

        def token(tk, par):
            xv, gv, ov = xs[par], gts[par], outs[par]
            for s in range(2 * SC_RING):
                q = s % SC_RING
                rows, gsem = ring[q], gsems[q]
                if s < SC_RING:
                    fetch(u_hbm, SC_RING * tk + q, rows, gsem).wait()
                    if s == 0:
                        xcopy(tk, par).wait()
                    dots(rows, xv, q * SC_ROWS)
                    fetch(v_hbm, SC_RING * tk + q, rows, gsem).start()
                    if s == SC_RING - 1:
                        gcopy(tk, par).wait()
                        weights(gv)
                else:
                    fetch(v_hbm, SC_RING * tk + q, rows, gsem).wait()
                    if s == SC_RING:
                        @pl.when(tk >= 2)
                        def _():
                            ocopy(tk - 2, par).wait()
                    combine(rows, q * SC_ROWS, ov, s == SC_RING)

                    @pl.when(tk + 1 < tokens)
                    def _():
                        fetch(u_hbm, SC_RING * (tk + 1) + q, rows, gsem).start()
                        if s == SC_RING:
                            xcopy(tk + 1, 1 - par).start()
                            gcopy(tk + 1, 1 - par).start()
            ocopy(tk, par).start()

        for q in range(SC_RING):
            fetch(u_hbm, q, ring[q], gsems[q]).start()
        xcopy(0, 0).start()
        gcopy(0, 0).start()

        @pl.loop(0, tokens, step=2)
        def _(tk):
            token(tk, 0)
            token(tk + 1, 1)

        ocopy(tokens - 2, 0).wait()
        ocopy(tokens - 1, 1).wait()

    return experts(u_tab, v_tab, idx.reshape(workers, tokens, PICKS), gates, h2p)


def _gate_matrix(idx, gates, n_experts):
    seq = idx.shape[0]
    info = plsc.get_sparse_core_info()
    lanes = info.num_lanes
    workers = info.num_cores * info.num_subcores
    tokens = seq // workers
    assert tokens * workers == seq and tokens % 2 == 0 and n_experts % lanes == 0
    mesh = plsc.VectorSubcoreMesh(core_axis_name="c", subcore_axis_name="s")
    row_t = pltpu.VMEM((n_experts,), jnp.float32)
    sem = pltpu.SemaphoreType.DMA

    @functools.partial(
        pl.kernel, mesh=mesh,
        compiler_params=pltpu.CompilerParams(needs_layout_passes=False),
        out_type=jax.ShapeDtypeStruct((seq, n_experts), jnp.float32),
        scratch_types=[pltpu.VMEM((tokens, PICKS), jnp.int32), pltpu.VMEM((tokens, PICKS), jnp.float32),
                       row_t, row_t, sem, sem],
    )
    def build(idx_hbm, gate_hbm, out_hbm, idx_v, gate_v, row0, row1, s0, s1):
        wid = lax.axis_index("s") * info.num_cores + lax.axis_index("c")
        tok0 = wid * tokens
        pltpu.sync_copy(idx_hbm.at[wid], idx_v)
        pltpu.sync_copy(gate_hbm.at[wid], gate_v)
        rows, sems = (row0, row1), (s0, s1)
        zeros = jnp.zeros((lanes,), jnp.float32)

        @pl.loop(0, n_experts, step=lanes)
        def _(i):
            row0[pl.ds(i, lanes)] = zeros
            row1[pl.ds(i, lanes)] = zeros

        def ocopy(tk, par):
            return pltpu.make_async_copy(rows[par], out_hbm.at[tok0 + tk], sems[par])

        def token(tk, par):
            row = rows[par]

            @pl.when(tk >= 2)
            def _():
                ocopy(tk - 2, par).wait()
                for c in range(PICKS // lanes):
                    plsc.store_scatter(row, [idx_v[tk - 2, pl.ds(c * lanes, lanes)]], zeros)
            for c in range(PICKS // lanes):
                sl = pl.ds(c * lanes, lanes)
                plsc.addupdate_scatter(row, [idx_v[tk, sl]], gate_v[tk, sl])
            ocopy(tk, par).start()

        @pl.loop(0, tokens, step=2)
        def _(tk):
            token(tk, 0)
            token(tk + 1, 1)

        ocopy(tokens - 2, 0).wait()
        ocopy(tokens - 1, 1).wait()

    shape = (workers, tokens, PICKS)
    return build(idx.reshape(shape), gates.reshape(shape))


def _dense_kernel(h2p_ref, g_ref, u_ref, v_ref, o_ref, h2_scr):
    half = h2p_ref.shape[1]

    @pl.when(pl.program_id(1) == 0)
    def _():
        words = h2p_ref[...]
        lo = lax.bitcast_convert_type(lax.shift_left(words, 16), jnp.float32)
        hi = lax.bitcast_convert_type(jnp.bitwise_and(words, jnp.int32(-65536)), jnp.float32)
        h2_scr[:, 0:half] = lo.astype(jnp.bfloat16)
        h2_scr[:, half:2 * half] = hi.astype(jnp.bfloat16)
        o_ref[...] = jnp.zeros_like(o_ref)

    s = lax.dot_general(h2_scr[...], u_ref[...], (((1,), (1,)), ((), ())),
                        preferred_element_type=jnp.float32)
    w = (g_ref[...] * _gelu(s)).astype(jnp.bfloat16)
    o_ref[...] += jnp.dot(w, v_ref[...], preferred_element_type=jnp.float32)


def _peer_dense(h2p, gmat, u_bf, v_bf):
    seq, half = h2p.shape
    n_experts, d = u_bf.shape
    tt = min(DENSE_TOKENS, seq)
    te = min(DENSE_EXPERTS, n_experts)
    return pl.pallas_call(
        _dense_kernel,
        grid=(seq // tt, n_experts // te),
        in_specs=[pl.BlockSpec((tt, half), lambda i, e: (i, 0)),
                  pl.BlockSpec((tt, te), lambda i, e: (i, e)),
                  pl.BlockSpec((te, d), lambda i, e: (e, 0)),
                  pl.BlockSpec((te, d), lambda i, e: (e, 0))],
        out_specs=pl.BlockSpec((tt, d), lambda i, e: (i, 0)),
        out_shape=jax.ShapeDtypeStruct((seq, d), jnp.float32),
        scratch_shapes=[pltpu.VMEM((tt, d), jnp.bfloat16)],
        compiler_params=pltpu.CompilerParams(vmem_limit_bytes=VMEM_LIMIT),
        name="peer_dense",
    )(h2p, gmat, u_bf, v_bf)


def _final_kernel(x_ref, peer_ref, g2_ref, g_ref, o_ref):
    o_ref[...] = _rms(x_ref[...] + g2_ref[...] * peer_ref[...], g_ref[...])


def _final_norm(x1, peer, gate2, gain):
    seq, d = x1.shape
    tt = min(1024, seq)
    tok_spec = pl.BlockSpec((tt, d), lambda i: (i, 0))
    return pl.pallas_call(
        _final_kernel,
        grid=(seq // tt,),
        in_specs=[tok_spec, tok_spec, _full(gate2), _full(gain)],
        out_specs=tok_spec,
        out_shape=jax.ShapeDtypeStruct((seq, d), jnp.float32),
        name="final_norm",
    )(x1, peer, gate2, gain)


def _pack_table(tab):
    half = tab.shape[1] // 2
    bits = lax.bitcast_convert_type(tab.astype(jnp.bfloat16), jnp.uint16).astype(jnp.uint32)
    words = bits[:, :half] | (bits[:, half:] << 16)
    return lax.bitcast_convert_type(words, jnp.int32)


def _pack_rows(tab):
    words = _pack_table(tab)
    return words.reshape(words.shape[0], words.shape[1] // LANES, LANES)


def kernel(x, c, ada_w, ada_b, norm_mix, norm_ffn, w_in, gm_ws, gm_bs, gm_vnorm, out_norm_a, out_norm_b,
           w_out, peer_wq, peer_k1, peer_k2, peer_u, peer_v, final_norm):
    batch, seq, d = x.shape
    depth = ada_w.shape[0]
    bf = jnp.bfloat16
    mod = _ada_mod(c, ada_w, ada_b).reshape(depth, batch, 6, d)
    state = [(x[b], None, None) for b in range(batch)]
    for l in range(depth):
        row = lambda a: a[l].reshape(1, -1)
        w_in_bf, w_out_bf, wq_bf = w_in[l].astype(bf), w_out[l].astype(bf), peer_wq[l].astype(bf)
        gm_bst = gm_bs[l].T
        u_pack = _pack_rows(peer_u[l])
        v_pack = _pack_rows(peer_v[l])
        u_bf, v_bf = peer_u[l].astype(bf), peer_v[l].astype(bf)
        for b in range(batch):
            mod_b = mod[l, b]
            ya, q, k, v, xb = _mix_in(*state[b], mod_b, row(norm_mix), w_in_bf, gm_ws[l], gm_bst,
                                      row(gm_vnorm), row(out_norm_a))
            yb = _stick_break(q, k, v)
            x1, h2p, eid, gate = _mix_out(xb, ya, yb, mod_b, row(out_norm_b), w_out_bf, row(norm_ffn),
                                         wq_bf, peer_k1[l], peer_k2[l])
            if b >= batch - DENSE_SEQS[l % len(DENSE_SEQS)]:
                peer = _peer_dense(h2p, _gate_matrix(eid, gate, u_bf.shape[0]), u_bf, v_bf)
            else:
                peer = _peer_experts(u_pack, v_pack, eid, gate, h2p)
            state[b] = (x1, peer, mod_b[5:6])
    gain = final_norm.reshape(1, d)
    return jnp.stack([_final_norm(*st, gain) for st in state], axis=0)
```
